```python
import math
import jax
import jax.numpy as jnp
from jax import lax
import numpy as np

D_MODEL = 1024
BATCH = 2
SEQ = 8192
DEPTH = 4

HEAD_DIM = 64
Q_BLOCK = 128
GLA_HEADS = 4
GLA_DK = 64
GLA_DV = 128
GLA_GATE_RANK = 16
GLA_GATE_TAU = 16.0
GLA_CHUNK = 64
DSA_HEADS = 8
IDX_HEADS = 8
IDX_DIM = 64
DSA_TOPK_MAX = 256
FOX_HEADS = 8
FORGET_BIAS_MEAN = 2.0
DIL_HEADS = 8
DIL_PATTERNS = ((128, 1), (512, 4), (2048, 16))
REL_BUCKETS = 32
REL_MAX_DIST = 2048
REL_HEADS = 8
D_FF = 2816
N_EXPERTS = 8
TOP_K = 2
D_FF_EXPERT = 3584
ALPHA = (2 * DEPTH) ** 0.25
BETA = (8 * DEPTH) ** -0.25
LN_EPS = 1e-5

AB_SPLITS = (GLA_HEADS * GLA_DK, GLA_HEADS * GLA_DK, GLA_HEADS * GLA_DV, GLA_HEADS * GLA_DV, GLA_GATE_RANK,
             DSA_HEADS * HEAD_DIM, DSA_HEADS * HEAD_DIM, DSA_HEADS * HEAD_DIM,
             IDX_HEADS * IDX_DIM, IDX_DIM, IDX_HEADS)
CD_SPLITS = (FOX_HEADS * HEAD_DIM, FOX_HEADS * HEAD_DIM, FOX_HEADS * HEAD_DIM, FOX_HEADS,
             DIL_HEADS * HEAD_DIM, DIL_HEADS * HEAD_DIM, DIL_HEADS * HEAD_DIM)
W_AB = sum(AB_SPLITS)
W_CD = sum(CD_SPLITS)
MIX_AB = GLA_HEADS * GLA_DV + DSA_HEADS * HEAD_DIM
MIX_CD = (FOX_HEADS + DIL_HEADS) * HEAD_DIM

kernel_name = "hybrid_gla_dsa_fox_dilated_moe"


def _split(y, sizes):
    idx = [int(v) for v in np.cumsum(sizes)[:-1]]
    return jnp.split(y, idx, axis=-1)


def _layernorm(x, g, b):
    xf = x.astype(jnp.float32)
    mu = jnp.mean(xf, axis=-1, keepdims=True)
    var = jnp.mean(jnp.square(xf - mu), axis=-1, keepdims=True)
    return ((xf - mu) * lax.rsqrt(var + LN_EPS) * g + b).astype(x.dtype)


def _rel_bucket(dist):
    max_exact = REL_BUCKETS // 2
    d = jnp.maximum(dist, 0)
    df = jnp.maximum(d, 1).astype(jnp.float32)
    large = max_exact + (jnp.log(df / max_exact) / math.log(REL_MAX_DIST / max_exact)
                         * (REL_BUCKETS - max_exact)).astype(jnp.int32)
    large = jnp.minimum(large, REL_BUCKETS - 1)
    return jnp.where(d < max_exact, d, large)


def _gla(q, k, v, g_log):
    B, L, H, dk = q.shape
    dv = v.shape[-1]
    C = GLA_CHUNK
    n = L // C

    def to_chunks(t):
        return t.reshape(B, n, C, H, t.shape[-1]).transpose(1, 0, 3, 2, 4)

    qc, kc, vc, gc = (to_chunks(q * dk ** -0.5), to_chunks(k), to_chunks(v), to_chunks(g_log))
    causal = jnp.tril(jnp.ones((C, C), dtype=bool))[None, None, :, :, None]

    def step(S, inp):
        qi, ki, vi, gi = inp
        G = jnp.cumsum(gi.astype(jnp.float32), axis=2)
        o_inter = jnp.einsum('bhck,bhkv->bhcv', qi * jnp.exp(G), S)
        diff = G[:, :, :, None, :] - G[:, :, None, :, :]
        decay = jnp.exp(jnp.where(causal, diff, -jnp.inf))
        A = jnp.einsum('bhtk,bhsk,bhtsk->bhts', qi, ki, decay)
        o_intra = jnp.einsum('bhts,bhsv->bhtv', A, vi)
        G_last = G[:, :, -1:, :]
        S_new = (jnp.exp(G_last[:, :, 0, :, None]) * S
                 + jnp.einsum('bhsk,bhsv->bhkv', ki * jnp.exp(G_last - G), vi))
        return S_new, o_inter + o_intra

    S0 = jnp.zeros((B, H, dk, dv), jnp.float32)
    _, o = lax.scan(step, S0, (qc, kc, vc, gc))
    return o.transpose(1, 0, 3, 2, 4).reshape(B, L, H, dv).astype(v.dtype)


def _dsa(q, k, v, q_idx, k_idx, w_idx, rel_table):
    B, L, H, dh = q.shape
    topk = min(DSA_TOPK_MAX, L // 4)
    pos = jnp.arange(L)
    gather = jax.vmap(lambda t, i: t[i])

    def block(i):
        q0 = i * Q_BLOCK
        tq = q0 + jnp.arange(Q_BLOCK)
        qib = lax.dynamic_slice_in_dim(q_idx, q0, Q_BLOCK, axis=1)
        wib = lax.dynamic_slice_in_dim(w_idx, q0, Q_BLOCK, axis=1)
        s = jnp.einsum('bqhd,bkd->bhqk', qib, k_idx).astype(jnp.float32)
        score = jnp.einsum('bqh,bhqk->bqk', wib.astype(jnp.float32), jax.nn.relu(s))
        causal = pos[None, :] <= tq[:, None]
        score = jnp.where(causal[None], score, -jnp.inf)
        _, sel = lax.top_k(score, topk)
        valid = sel <= tq[None, :, None]
        ksel = gather(k, sel)
        vsel = gather(v, sel)
        qb = lax.dynamic_slice_in_dim(q, q0, Q_BLOCK, axis=1)
        logits = jnp.einsum('bqhd,bqkhd->bhqk', qb, ksel).astype(jnp.float32) * dh ** -0.5
        bias = rel_table[_rel_bucket(tq[None, :, None] - sel)]
        logits = logits + bias.transpose(0, 3, 1, 2).astype(jnp.float32)
        logits = jnp.where(valid[:, None], logits, -jnp.inf)
        p = jax.nn.softmax(logits, axis=-1)
        return jnp.einsum('bhqk,bqkhd->bqhd', p.astype(v.dtype), vsel)

    out = lax.map(block, jnp.arange(L // Q_BLOCK))
    return out.transpose(1, 0, 2, 3, 4).reshape(B, L, H, dh)


def _fox(q, k, v, log_f):
    B, L, H, dh = q.shape
    F = jnp.cumsum(log_f, axis=1).transpose(0, 2, 1)
    pos = jnp.arange(L)

    def block(i):
        q0 = i * Q_BLOCK
        tq = q0 + jnp.arange(Q_BLOCK)
        qb = lax.dynamic_slice_in_dim(q, q0, Q_BLOCK, axis=1)
        Fq = lax.dynamic_slice_in_dim(F, q0, Q_BLOCK, axis=2)
        logits = (jnp.einsum('bqhd,bkhd->bhqk', qb, k).astype(jnp.float32) * dh ** -0.5
                  + Fq[..., None] - F[:, :, None, :])
        causal = pos[None, :] <= tq[:, None]
        logits = jnp.where(causal[None, None], logits, -jnp.inf)
        p = jax.nn.softmax(logits, axis=-1)
        return jnp.einsum('bhqk,bkhd->bqhd', p.astype(v.dtype), v)

    out = lax.map(block, jnp.arange(L // Q_BLOCK))
    return out.transpose(1, 0, 2, 3, 4).reshape(B, L, H, dh)


def _dilated(q, k, v, rel_table):
    B, L, H, dh = q.shape

    def block(i):
        q0 = i * Q_BLOCK
        tq = q0 + jnp.arange(Q_BLOCK)
        qb = lax.dynamic_slice_in_dim(q, q0, Q_BLOCK, axis=1)
        outs, log_den = [], []
        for window, dil in DIL_PATTERNS:
            offs = dil * jnp.arange(window // dil + 1)
            idx = tq[:, None] - offs[None, :]
            valid = idx >= 0
            idxc = jnp.maximum(idx, 0)
            ksel = k[:, idxc]
            vsel = v[:, idxc]
            bias = rel_table[_rel_bucket(offs)].T.astype(jnp.float32)
            lg = (jnp.einsum('bqhd,bqjhd->bhqj', qb, ksel).astype(jnp.float32) * dh ** -0.5
                  + bias[None, :, None, :])
            lg = jnp.where(valid[None, None], lg, -jnp.inf)
            m = jnp.max(lg, axis=-1, keepdims=True)
            e = jnp.exp(lg - m)
            s = jnp.sum(e, axis=-1, keepdims=True)
            outs.append(jnp.einsum('bhqj,bqjhd->bhqd', e, vsel.astype(jnp.float32)) / s)
            log_den.append(m + jnp.log(s))
        wts = jax.nn.softmax(jnp.stack(log_den, 0), axis=0)
        o = jnp.sum(wts * jnp.stack(outs, 0), axis=0)
        return o.transpose(0, 2, 1, 3).astype(v.dtype)

    out = lax.map(block, jnp.arange(L // Q_BLOCK))
    return out.transpose(1, 0, 2, 3, 4).reshape(B, L, H, dh)


def _mixer_ab(x, w_in, w_gate, b_gate, g_norm, w_out, rel_table):
    B, L, _ = x.shape
    y = x @ w_in
    qa, ka, va, ra, ga, qb, kb, vb, qi, ki, wi = _split(y, AB_SPLITS)
    g_log = jax.nn.log_sigmoid((ga @ w_gate + b_gate).astype(jnp.float32)) / GLA_GATE_TAU
    oa = _gla(qa.reshape(B, L, GLA_HEADS, GLA_DK), ka.reshape(B, L, GLA_HEADS, GLA_DK),
              va.reshape(B, L, GLA_HEADS, GLA_DV), g_log.reshape(B, L, GLA_HEADS, GLA_DK))
    of = oa.astype(jnp.float32)
    of = of * lax.rsqrt(jnp.mean(jnp.square(of), axis=-1, keepdims=True) + LN_EPS) * g_norm
    oa = of.astype(x.dtype) * jax.nn.silu(ra.reshape(B, L, GLA_HEADS, GLA_DV))
    ob = _dsa(qb.reshape(B, L, DSA_HEADS, HEAD_DIM), kb.reshape(B, L, DSA_HEADS, HEAD_DIM),
              vb.reshape(B, L, DSA_HEADS, HEAD_DIM), qi.reshape(B, L, IDX_HEADS, IDX_DIM), ki, wi, rel_table)
    o = jnp.concatenate([oa.reshape(B, L, -1), ob.reshape(B, L, -1)], axis=-1)
    return o @ w_out


def _mixer_cd(x, w_in, b_forget, w_out, rel_table):
    B, L, _ = x.shape
    y = x @ w_in
    qc, kc, vc, fc, qd, kd, vd = _split(y, CD_SPLITS)
    log_f = jax.nn.log_sigmoid((fc + b_forget).astype(jnp.float32))
    oc = _fox(qc.reshape(B, L, FOX_HEADS, HEAD_DIM), kc.reshape(B, L, FOX_HEADS, HEAD_DIM),
              vc.reshape(B, L, FOX_HEADS, HEAD_DIM), log_f)
    od = _dilated(qd.reshape(B, L, DIL_HEADS, HEAD_DIM), kd.reshape(B, L, DIL_HEADS, HEAD_DIM),
                  vd.reshape(B, L, DIL_HEADS, HEAD_DIM), rel_table)
    o = jnp.concatenate([oc.reshape(B, L, -1), od.reshape(B, L, -1)], axis=-1)
    return o @ w_out


def _swiglu(x, w1, w3, w2):
    return (jax.nn.silu(x @ w1) * (x @ w3)) @ w2


def _moe(x, w_router, w1, w3, w2):
    logits = (x @ w_router).astype(jnp.float32)
    top_v, top_i = lax.top_k(logits, TOP_K)
    gates = jax.nn.softmax(top_v, axis=-1)
    combine = jnp.sum(jax.nn.one_hot(top_i, N_EXPERTS, dtype=jnp.float32) * gates[..., None], axis=-2)
    combine = combine.astype(x.dtype)
    y = jnp.zeros_like(x)
    for e in range(N_EXPERTS):
        y = y + _swiglu(x, w1[e], w3[e], w2[e]) * combine[..., e:e + 1]
    return y


def setup_inputs(seed: int = 0) -> dict:
    key = jax.random.key(seed)
    ks = iter(jax.random.split(key, 32))
    n_e = (DEPTH + 1) // 2
    n_o = DEPTH // 2

    def nrm(shape, scale):
        return jax.random.normal(next(ks), shape, jnp.float32) * scale

    return {
        "x": nrm((BATCH, SEQ, D_MODEL), 1.0),
        "ln_g": 1.0 + nrm((DEPTH, 2, D_MODEL), 0.02),
        "ln_b": nrm((DEPTH, 2, D_MODEL), 0.02),
        "rel_table": nrm((REL_BUCKETS, REL_HEADS), 0.2),
        "w_in_ab": nrm((n_e, D_MODEL, W_AB), D_MODEL ** -0.5),
        "w_gate_a": nrm((n_e, GLA_GATE_RANK, GLA_HEADS * GLA_DK), GLA_GATE_RANK ** -0.5),
        "b_gate_a": nrm((n_e, GLA_HEADS * GLA_DK), 0.1),
        "g_norm_a": 1.0 + nrm((n_e, GLA_DV), 0.02),
        "w_out_ab": nrm((n_e, MIX_AB, D_MODEL), MIX_AB ** -0.5 * BETA),
        "w_in_cd": nrm((n_o, D_MODEL, W_CD), D_MODEL ** -0.5),
        "b_forget": FORGET_BIAS_MEAN + nrm((n_o, FOX_HEADS), 0.1),
        "w_out_cd": nrm((n_o, MIX_CD, D_MODEL), MIX_CD ** -0.5 * BETA),
        "w1_dense": nrm((n_e, D_MODEL, D_FF), D_MODEL ** -0.5),
        "w3_dense": nrm((n_e, D_MODEL, D_FF), D_MODEL ** -0.5),
        "w2_dense": nrm((n_e, D_FF, D_MODEL), D_FF ** -0.5 * BETA),
        "w_router": nrm((n_o, D_MODEL, N_EXPERTS), D_MODEL ** -0.5),
        "w1_moe": nrm((n_o, N_EXPERTS, D_MODEL, D_FF_EXPERT), D_MODEL ** -0.5),
        "w3_moe": nrm((n_o, N_EXPERTS, D_MODEL, D_FF_EXPERT), D_MODEL ** -0.5),
        "w2_moe": nrm((n_o, N_EXPERTS, D_FF_EXPERT, D_MODEL), D_FF_EXPERT ** -0.5 * BETA),
    }


def reference(x, ln_g, ln_b, rel_table, w_in_ab, w_gate_a, b_gate_a, g_norm_a, w_out_ab,
              w_in_cd, b_forget, w_out_cd, w1_dense, w3_dense, w2_dense,
              w_router, w1_moe, w3_moe, w2_moe):
    for layer in range(DEPTH):
        j = layer // 2
        if layer % 2 == 0:
            mix = _mixer_ab(x, w_in_ab[j], w_gate_a[j], b_gate_a[j], g_norm_a[j], w_out_ab[j], rel_table)
        else:
            mix = _mixer_cd(x, w_in_cd[j], b_forget[j], w_out_cd[j], rel_table)
        x = _layernorm(ALPHA * x + mix, ln_g[layer, 0], ln_b[layer, 0])
        if layer % 2 == 0:
            ffn = _swiglu(x, w1_dense[j], w3_dense[j], w2_dense[j])
        else:
            ffn = _moe(x, w_router[j], w1_moe[j], w3_moe[j], w2_moe[j])
        x = _layernorm(ALPHA * x + ffn, ln_g[layer, 1], ln_b[layer, 1])
    return x
```

```python
import functools
import math

import jax
import jax.numpy as jnp
from jax import lax
from jax.experimental import pallas as pl
from jax.experimental.pallas import tpu as pltpu

D_MODEL = 1024
DEPTH = 4
HEAD_DIM = 64
GLA_HEADS = 4
GLA_DK = 64
GLA_DV = 128
GLA_GATE_RANK = 16
GLA_GATE_TAU = 16.0
GLA_CHUNK = 64
DSA_HEADS = 8
IDX_HEADS = 8
IDX_DIM = 64
DSA_TOPK_MAX = 256
FOX_HEADS = 8
DIL_HEADS = 8
DIL_PATTERNS = ((128, 1), (512, 4), (2048, 16))
REL_BUCKETS = 32
REL_MAX_DIST = 2048
REL_HEADS = 8
D_FF = 2816
N_EXPERTS = 8
TOP_K = 2
D_FF_EXPERT = 3584
ALPHA = (2 * DEPTH) ** 0.25
LN_EPS = 1e-5

LANES = 128
MASKED = -1e30
M_INIT = -1e20
VMEM_LIMIT = 56 * 1024 * 1024

F32 = jnp.float32
BF16 = jnp.bfloat16

AB_QA, AB_KA, AB_VA, AB_RA, AB_QB, AB_KB, AB_VB, AB_QI = 0, 256, 512, 1024, 1536, 2048, 2560, 3072
AB_MAIN = 3584
AB_EXTRA = 256
EX_GA = 0
EX_WI = 16
CD_QC, CD_KC, CD_VC, CD_QD, CD_KD, CD_VD = 0, 512, 1024, 1536, 2048, 2560
CD_MAIN = 3072
CD_EXTRA = 128


def _cparams(sem, vmem=None):
    return pltpu.CompilerParams(dimension_semantics=sem, vmem_limit_bytes=vmem)


def _mm_kernel(x_ref, w_ref, o_ref):
    o_ref[...] = jnp.dot(x_ref[...], w_ref[...], preferred_element_type=F32).astype(o_ref.dtype)


def _matmul(x, w, out_dtype, tm, tn):
    n, k = x.shape
    m = w.shape[1]
    return pl.pallas_call(
        _mm_kernel,
        grid=(n // tm, m // tn),
        in_specs=[pl.BlockSpec((tm, k), lambda i, j: (i, 0)),
                  pl.BlockSpec((k, tn), lambda i, j: (0, j))],
        out_specs=pl.BlockSpec((tm, tn), lambda i, j: (i, j)),
        out_shape=jax.ShapeDtypeStruct((n, m), out_dtype),
        compiler_params=_cparams(("parallel", "parallel"), VMEM_LIMIT),
        name="proj_matmul",
    )(x, w)


def _layernorm_rows(z, g, b):
    mu = jnp.mean(z, axis=-1, keepdims=True)
    zc = z - mu
    var = jnp.mean(zc * zc, axis=-1, keepdims=True)
    return zc * lax.rsqrt(var + LN_EPS) * g + b


def _outproj_ln_kernel(o_ref, w_ref, x_ref, g_ref, b_ref, xf_ref, xb_ref):
    mix = jnp.dot(o_ref[...], w_ref[...], preferred_element_type=F32)
    y = _layernorm_rows(ALPHA * x_ref[...] + mix, g_ref[...], b_ref[...])
    xf_ref[...] = y
    xb_ref[...] = y.astype(BF16)


def _outproj_ln(o, w, x, g, b, tm=256):
    n, k = o.shape
    d = w.shape[1]
    return pl.pallas_call(
        _outproj_ln_kernel,
        grid=(n // tm,),
        in_specs=[pl.BlockSpec((tm, k), lambda i: (i, 0)),
                  pl.BlockSpec((k, d), lambda i: (0, 0)),
                  pl.BlockSpec((tm, d), lambda i: (i, 0)),
                  pl.BlockSpec((1, d), lambda i: (0, 0)),
                  pl.BlockSpec((1, d), lambda i: (0, 0))],
        out_specs=[pl.BlockSpec((tm, d), lambda i: (i, 0)),
                   pl.BlockSpec((tm, d), lambda i: (i, 0))],
        out_shape=[jax.ShapeDtypeStruct((n, d), F32), jax.ShapeDtypeStruct((n, d), BF16)],
        compiler_params=_cparams(("parallel",), VMEM_LIMIT),
        name="outproj_ln",
    )(o, w, x, g, b)


def _silu(a):
    return a / (1.0 + jnp.exp(-a))


def _swiglu_ln_kernel(xb_ref, w1_ref, w3_ref, w2_ref, x_ref, g_ref, b_ref, xf_ref, xbo_ref, acc_ref):
    f = pl.program_id(1)

    @pl.when(f == 0)
    def _():
        acc_ref[...] = jnp.zeros_like(acc_ref)

    xb = xb_ref[...]
    a = jnp.dot(xb, w1_ref[...], preferred_element_type=F32)
    c = jnp.dot(xb, w3_ref[...], preferred_element_type=F32)
    h = (_silu(a) * c).astype(BF16)
    acc_ref[...] += jnp.dot(h, w2_ref[...], preferred_element_type=F32)

    @pl.when(f == pl.num_programs(1) - 1)
    def _():
        y = _layernorm_rows(ALPHA * x_ref[...] + acc_ref[...], g_ref[...], b_ref[...])
        xf_ref[...] = y
        xbo_ref[...] = y.astype(BF16)


def _swiglu_ln(xb, w1, w3, w2, x, g, b, tm=512, tf=1408):
    n, d = xb.shape
    ff = w1.shape[1]
    return pl.pallas_call(
        _swiglu_ln_kernel,
        grid=(n // tm, ff // tf),
        in_specs=[pl.BlockSpec((tm, d), lambda i, f: (i, 0)),
                  pl.BlockSpec((d, tf), lambda i, f: (0, f)),
                  pl.BlockSpec((d, tf), lambda i, f: (0, f)),
                  pl.BlockSpec((tf, d), lambda i, f: (f, 0)),
                  pl.BlockSpec((tm, d), lambda i, f: (i, 0)),
                  pl.BlockSpec((1, d), lambda i, f: (0, 0)),
                  pl.BlockSpec((1, d), lambda i, f: (0, 0))],
        out_specs=[pl.BlockSpec((tm, d), lambda i, f: (i, 0)),
                   pl.BlockSpec((tm, d), lambda i, f: (i, 0))],
        out_shape=[jax.ShapeDtypeStruct((n, d), F32), jax.ShapeDtypeStruct((n, d), BF16)],
        scratch_shapes=[pltpu.VMEM((tm, d), F32)],
        compiler_params=_cparams(("parallel", "arbitrary"), VMEM_LIMIT),
        name="swiglu_ln",
    )(xb, w1, w3, w2, x, g, b)


BVEC_LEN = 2304
BVEC_SHIFT = 127


def _bvec_kernel(tab_ref, o_ref):
    n = lax.broadcasted_iota(jnp.int32, (REL_HEADS, BVEC_LEN), 1)
    d = jnp.maximum(n - BVEC_SHIFT, 0)
    max_exact = REL_BUCKETS // 2
    df = jnp.maximum(d, 1).astype(F32)
    large = max_exact + (jnp.log(df / max_exact) / math.log(REL_MAX_DIST / max_exact)
                         * (REL_BUCKETS - max_exact)).astype(jnp.int32)
    large = jnp.minimum(large, REL_BUCKETS - 1)
    bucket = jnp.where(d < max_exact, d, large)
    acc = jnp.zeros((REL_HEADS, BVEC_LEN), F32)
    for k in range(REL_BUCKETS):
        acc = jnp.where(bucket == k, tab_ref[:, k:k + 1], acc)
    o_ref[...] = acc


def _bias_vector(rel_table):
    return pl.pallas_call(
        _bvec_kernel,
        out_shape=jax.ShapeDtypeStruct((REL_HEADS, BVEC_LEN), F32),
        name="t5_bias_vector",
    )(rel_table.T)


GLA_TB = 256


def _log_sigmoid(z):
    return jnp.minimum(z, 0.0) - jnp.log(1.0 + jnp.exp(-jnp.abs(z)))


def _cumsum_rows(x):
    rows = x.shape[0]
    row = lax.broadcasted_iota(jnp.int32, x.shape, 0)
    sh = 1
    while sh < rows:
        x = x + jnp.where(row >= sh, pltpu.roll(x, sh, 0), 0.0)
        sh *= 2
    return x


def _gla_kernel(q_ref, k_ref, v_ref, r_ref, ga_ref, wg_ref, bg_ref, gn_ref, o_ref, s_ref, a_ref):
    c_id = pl.program_id(2)

    @pl.when(c_id == 0)
    def _():
        s_ref[...] = jnp.zeros_like(s_ref)

    C = GLA_CHUNK
    srow = lax.broadcasted_iota(jnp.int32, (2 * GLA_DK, 2 * GLA_DV), 0)
    scol = lax.broadcasted_iota(jnp.int32, (2 * GLA_DK, 2 * GLA_DV), 1)
    blockdiag = (srow < GLA_DK) == (scol < GLA_DV)
    trow = lax.broadcasted_iota(jnp.int32, (C, C), 0)
    tcol = lax.broadcasted_iota(jnp.int32, (C, C), 1)
    causal = tcol <= trow

    def chunk(c, carry):
        r0 = pl.multiple_of(c * C, C)
        q2 = q_ref[pl.ds(r0, C), :].astype(F32) * (GLA_DK ** -0.5)
        k2 = k_ref[pl.ds(r0, C), :].astype(F32)
        v2 = v_ref[pl.ds(r0, C), :]
        ga = ga_ref[pl.ds(r0, C), :][:, EX_GA:EX_GA + GLA_GATE_RANK]
        z = jnp.dot(ga.astype(BF16), wg_ref[...].astype(BF16), preferred_element_type=F32) + bg_ref[...]
        g = _log_sigmoid(z) / GLA_GATE_TAU
        G = _cumsum_rows(g)
        GT = G.T
        kT = k2.T
        qT = q2.T
        for t in range(C):
            dec = jnp.exp(GT[:, t:t + 1] - GT)
            prod = (qT[:, t:t + 1] * kT) * dec
            a_ref[0, t:t + 1, :] = jnp.sum(prod[:GLA_DK], axis=0, keepdims=True)
            a_ref[1, t:t + 1, :] = jnp.sum(prod[GLA_DK:], axis=0, keepdims=True)
        S = s_ref[...]
        o_inter = jnp.dot((q2 * jnp.exp(G)).astype(BF16), S.astype(BF16), preferred_element_type=F32)
        o_intra = []
        for h in range(2):
            A = jnp.where(causal, a_ref[h], 0.0)
            o_intra.append(jnp.dot(A.astype(BF16), v2[:, h * GLA_DV:(h + 1) * GLA_DV],
                                   preferred_element_type=F32))
        o2 = o_inter + jnp.concatenate(o_intra, axis=1)
        g_last = GT[:, C - 1:C]
        kd = kT * jnp.exp(g_last - GT)
        upd = jnp.dot(kd.astype(BF16), v2, preferred_element_type=F32)
        s_ref[...] = jnp.exp(g_last) * S + jnp.where(blockdiag, upd, 0.0)
        outs = []
        for h in range(2):
            of = o2[:, h * GLA_DV:(h + 1) * GLA_DV]
            of = of * lax.rsqrt(jnp.mean(of * of, axis=-1, keepdims=True) + LN_EPS) * gn_ref[...]
            rr = r_ref[pl.ds(r0, C), h * GLA_DV:(h + 1) * GLA_DV].astype(F32)
            outs.append(of * _silu(rr))
        o_ref[pl.ds(r0, C), :] = jnp.concatenate(outs, axis=1).astype(o_ref.dtype)
        return carry

    lax.fori_loop(0, GLA_TB // C, chunk, 0)


def _gla(main, extra, w_gate, b_gate, g_norm, batch, seq):
    nblk = seq // GLA_TB
    n = batch * seq

    def rows(b, hp, c):
        return b * nblk + c

    return pl.pallas_call(
        _gla_kernel,
        grid=(batch, 2, nblk),
        in_specs=[
            pl.BlockSpec((GLA_TB, 128), lambda b, hp, c: (rows(b, hp, c), AB_QA // 128 + hp)),
            pl.BlockSpec((GLA_TB, 128), lambda b, hp, c: (rows(b, hp, c), AB_KA // 128 + hp)),
            pl.BlockSpec((GLA_TB, 256), lambda b, hp, c: (rows(b, hp, c), AB_VA // 256 + hp)),
            pl.BlockSpec((GLA_TB, 256), lambda b, hp, c: (rows(b, hp, c), AB_RA // 256 + hp)),
            pl.BlockSpec((GLA_TB, 128), lambda b, hp, c: (rows(b, hp, c), 1)),
            pl.BlockSpec((GLA_GATE_RANK, 128), lambda b, hp, c: (0, hp)),
            pl.BlockSpec((1, 128), lambda b, hp, c: (0, hp)),
            pl.BlockSpec((1, GLA_DV), lambda b, hp, c: (0, 0)),
        ],
        out_specs=pl.BlockSpec((GLA_TB, 256), lambda b, hp, c: (rows(b, hp, c), hp)),
        out_shape=jax.ShapeDtypeStruct((n, GLA_HEADS * GLA_DV), BF16),
        scratch_shapes=[pltpu.VMEM((2 * GLA_DK, 2 * GLA_DV), F32),
                        pltpu.VMEM((2, GLA_CHUNK, GLA_CHUNK), F32)],
        compiler_params=_cparams(("parallel", "parallel", "arbitrary"), VMEM_LIMIT),
        name="gla",
    )(main, main, main, main, extra, w_gate, b_gate, g_norm)


ATT_TQ = 256
ATT_TK = 512


def _half_mask(shape, half):
    lane = lax.broadcasted_iota(jnp.int32, shape, len(shape) - 1)
    return (lane < HEAD_DIM) if half == 0 else (lane >= HEAD_DIM)


def _flash_step(s, v2, m, l, acc):
    m_new = jnp.maximum(m, jnp.max(s, axis=-1, keepdims=True))
    alpha = jnp.exp(m - m_new)
    p = jnp.exp(s - m_new)
    l = alpha * l + jnp.sum(p, axis=-1, keepdims=True)
    acc = alpha * acc + jnp.dot(p.astype(BF16), v2, preferred_element_type=F32)
    return m_new, l, acc


def _qk(qm, k2):
    return lax.dot_general(qm, k2, (((1,), (1,)), ((), ())), preferred_element_type=F32) * (HEAD_DIM ** -0.5)


def _load_kv(b, seq, srcs, dsts, sems):
    copies = []
    for n, (src, col, width) in enumerate(srcs):
        cp = pltpu.make_async_copy(src.at[pl.ds(b * seq, seq), pl.ds(col, width)], dsts[n], sems.at[n])
        cp.start()
        copies.append(cp)
    for cp in copies:
        cp.wait()


def _fcum_kernel(ex_ref, bf_ref, o_ref, carry_ref):
    i = pl.program_id(1)

    @pl.when(i == 0)
    def _():
        carry_ref[...] = jnp.zeros_like(carry_ref)

    lf = _log_sigmoid(ex_ref[...] + bf_ref[...])
    F = _cumsum_rows(lf) + carry_ref[...]
    o_ref[...] = F
    carry_ref[...] = F[F.shape[0] - 1:, :]


def _forget_cumsum(extra, b_forget_row, batch, seq, tb=256):
    nblk = seq // tb
    return pl.pallas_call(
        _fcum_kernel,
        grid=(batch, nblk),
        in_specs=[pl.BlockSpec((tb, 128), lambda b, i: (b * nblk + i, 0)),
                  pl.BlockSpec((1, 128), lambda b, i: (0, 0))],
        out_specs=pl.BlockSpec((tb, 128), lambda b, i: (b * nblk + i, 0)),
        out_shape=jax.ShapeDtypeStruct((batch * seq, 128), F32),
        scratch_shapes=[pltpu.VMEM((1, 128), F32)],
        compiler_params=_cparams(("parallel", "arbitrary")),
        name="forget_cumsum",
    )(extra, b_forget_row)


def _fox_kernel(q_ref, fq_ref, fk_ref, kv_hbm, o_ref, k_vm, v_vm, sems, *, seq):
    b = pl.program_id(0)
    i = pl.program_id(1)
    TQ, TK = ATT_TQ, ATT_TK

    @pl.when(i == 0)
    def _():
        _load_kv(b, seq, [(kv_hbm, CD_KC, 512), (kv_hbm, CD_VC, 512)], [k_vm, v_vm], sems)

    nj = (i * TQ + TQ - 1) // TK + 1
    row = i * TQ + lax.broadcasted_iota(jnp.int32, (TQ, TK), 0)
    col0 = lax.broadcasted_iota(jnp.int32, (TQ, TK), 1)
    for p in range(FOX_HEADS // 2):
        q2 = q_ref[:, p * 128:(p + 1) * 128]
        pair = []
        for half in range(2):
            h = 2 * p + half
            qm = jnp.where(_half_mask((TQ, 128), half), q2, jnp.zeros_like(q2))
            fq = fq_ref[:, h:h + 1]

            def body(j, carry, qm=qm, fq=fq, h=h, p=p):
                m, l, acc = carry
                k0 = pl.multiple_of(j * TK, TK)
                k2 = k_vm[pl.ds(k0, TK), p * 128:(p + 1) * 128]
                v2 = v_vm[pl.ds(k0, TK), p * 128:(p + 1) * 128]
                s = _qk(qm, k2) + fq - fk_ref[h, pl.ds(j, 1), :]
                s = jnp.where(col0 + j * TK <= row, s, MASKED)
                return _flash_step(s, v2, m, l, acc)

            init = (jnp.full((TQ, 1), M_INIT, F32), jnp.zeros((TQ, 1), F32), jnp.zeros((TQ, 128), F32))
            m, l, acc = lax.fori_loop(0, nj, body, init)
            pair.append(acc / l)
        o_ref[:, p * 128:(p + 1) * 128] = jnp.where(_half_mask((TQ, 128), 0), pair[0], pair[1]).astype(o_ref.dtype)


def _fox(main, fcum, batch, seq):
    TQ, TK = ATT_TQ, ATT_TK
    nq = seq // TQ
    nk = seq // TK
    n = batch * seq
    fk = fcum[:, :FOX_HEADS].reshape(batch, seq, FOX_HEADS).transpose(0, 2, 1).reshape(batch * FOX_HEADS, nk, TK)
    return pl.pallas_call(
        functools.partial(_fox_kernel, seq=seq),
        grid=(batch, nq),
        in_specs=[pl.BlockSpec((TQ, 512), lambda b, i: (b * nq + i, CD_QC // 512)),
                  pl.BlockSpec((TQ, 128), lambda b, i: (b * nq + i, 0)),
                  pl.BlockSpec((FOX_HEADS, nk, TK), lambda b, i: (b, 0, 0)),
                  pl.BlockSpec(memory_space=pl.ANY)],
        out_specs=pl.BlockSpec((TQ, 512), lambda b, i: (b * nq + i, 0)),
        out_shape=jax.ShapeDtypeStruct((n, 512), BF16),
        scratch_shapes=[pltpu.VMEM((seq, 512), BF16), pltpu.VMEM((seq, 512), BF16),
                        pltpu.SemaphoreType.DMA((2,))],
        compiler_params=_cparams(("arbitrary", "arbitrary"), VMEM_LIMIT),
        name="fox_attention",
    )(main, fcum, fk, main)


INT_MIN = -(2 ** 31)
N_DELTA = 18


def _sortable(x):
    b = pltpu.bitcast(x, jnp.int32)
    return b ^ ((b >> 31) & 0x7FFFFFFF)


def _fold_lanes(x):
    out = x[:, 0:LANES]
    for c in range(1, x.shape[1] // LANES):
        out = out + x[:, c * LANES:(c + 1) * LANES]
    return out


def _dsa_kernel(q_ref, qi_ref, ex_ref, kv_hbm, kid_hbm, tz_hbm, o_ref,
                k_vm, v_vm, kid_vm, tz_vm, key_vm, qm_vm, qim_vm, mask_vm, m_vm, l_vm, acc_vm, sems,
                *, seq, topk):
    b = pl.program_id(0)
    i = pl.program_id(1)
    TQ, TK = ATT_TQ, ATT_TK

    @pl.when(i == 0)
    def _():
        _load_kv(b, seq, [(kv_hbm, AB_KB, 512), (kv_hbm, AB_VB, 512), (kid_hbm, 0, 128)],
                 [k_vm, v_vm, kid_vm], sems)

    @pl.when((i == 0) & (b == 0))
    def _():
        cp = pltpu.make_async_copy(tz_hbm, tz_vm, sems.at[3])
        cp.start()
        cp.wait()

    nj = (i * TQ + TQ - 1) // TK + 1
    row = i * TQ + lax.broadcasted_iota(jnp.int32, (TQ, TK), 0)
    col0 = lax.broadcasted_iota(jnp.int32, (TQ, TK), 1)

    for p in range(4):
        q2 = q_ref[:, p * 128:(p + 1) * 128]
        qi2 = qi_ref[:, p * 128:(p + 1) * 128]
        for half in range(2):
            keep = _half_mask((TQ, 128), half)
            qm_vm[2 * p + half] = jnp.where(keep, q2, jnp.zeros_like(q2))
            qim_vm[2 * p + half] = jnp.where(keep, qi2, jnp.zeros_like(qi2))

    def score_chunk(j, carry):
        k0 = pl.multiple_of(j * TK, TK)
        kd = kid_vm[pl.ds(k0, TK), :].astype(BF16)
        score = jnp.zeros((TQ, TK), F32)
        for h in range(IDX_HEADS):
            s = lax.dot_general(qim_vm[h], kd, (((1,), (1,)), ((), ())), preferred_element_type=F32)
            w = ex_ref[:, EX_WI + h:EX_WI + h + 1]
            score = score + w * jnp.maximum(s, 0.0)
        score = jnp.where(col0 + j * TK <= row, score, -jnp.inf)
        key_vm[j] = _sortable(score)
        return carry

    lax.fori_loop(0, nj, score_chunk, 0)

    def count_ge(c):
        def body(j, acc):
            return acc + _fold_lanes(jnp.where(key_vm[j] >= c, 1, 0))
        acc = lax.fori_loop(0, nj, body, jnp.zeros((TQ, LANES), jnp.int32))
        return jnp.sum(acc, axis=-1, keepdims=True)

    ans = jnp.where(count_ge(jnp.zeros((TQ, 1), jnp.int32)) >= topk, 0, INT_MIN).astype(jnp.int32)

    def bit_step(it, ans):
        cand = ans | (jnp.int32(1) << (30 - it))
        return jnp.where(count_ge(cand) >= topk, cand, ans)

    thr = lax.fori_loop(0, 31, bit_step, ans)

    need = topk - count_ge(thr + 1)

    def count_eq_below(x):
        def body(j, acc):
            hit = (key_vm[j] == thr) & (col0 + j * TK < x)
            return acc + _fold_lanes(jnp.where(hit, 1, 0))
        acc = lax.fori_loop(0, nj, body, jnp.zeros((TQ, LANES), jnp.int32))
        return jnp.sum(acc, axis=-1, keepdims=True)

    nbits = max(1, (seq - 1).bit_length())

    def cut_step(it, x):
        cand = x | (jnp.int32(1) << (nbits - 1 - it))
        return jnp.where(count_eq_below(cand) < need, cand, x)

    cut = lax.fori_loop(0, nbits, cut_step, jnp.zeros((TQ, 1), jnp.int32))

    m_vm[...] = jnp.full(m_vm.shape, M_INIT, F32)
    l_vm[...] = jnp.zeros(l_vm.shape, F32)
    acc_vm[...] = jnp.zeros(acc_vm.shape, F32)

    def attend(j, carry):
        k0 = pl.multiple_of(j * TK, TK)
        col = col0 + j * TK
        key = key_vm[j]
        sel = ((key > thr) | ((key == thr) & (col <= cut))) & (col <= row)
        mask_vm[...] = jnp.where(sel, 0.0, MASKED)
        for h in range(DSA_HEADS):
            p = h // 2
            k2 = k_vm[pl.ds(k0, TK), p * 128:(p + 1) * 128]
            v2 = v_vm[pl.ds(k0, TK), p * 128:(p + 1) * 128]
            tiles = []
            for ra in range(TQ // 128):
                rowt = []
                for cb in range(TK // 128):
                    delta = jnp.clip(i * (TQ // 128) + ra - j * (TK // 128) - cb, 0, N_DELTA - 1)
                    rowt.append(tz_vm[delta, h])
                tiles.append(jnp.concatenate(rowt, axis=1))
            bias = jnp.concatenate(tiles, axis=0)
            s = _qk(qm_vm[h], k2) + bias + mask_vm[...]
            m, l, acc = _flash_step(s, v2, m_vm[h], l_vm[h], acc_vm[h])
            m_vm[h] = m
            l_vm[h] = l
            acc_vm[h] = acc
        return carry

    lax.fori_loop(0, nj, attend, 0)

    for p in range(4):
        o0 = acc_vm[2 * p] / l_vm[2 * p]
        o1 = acc_vm[2 * p + 1] / l_vm[2 * p + 1]
        o_ref[:, p * 128:(p + 1) * 128] = jnp.where(_half_mask((TQ, 128), 0), o0, o1).astype(o_ref.dtype)


def _dsa(main, extra, toeplitz, batch, seq):
    TQ, TK = ATT_TQ, ATT_TK
    nq = seq // TQ
    nk = seq // TK
    n = batch * seq
    topk = min(DSA_TOPK_MAX, seq // 4)
    return pl.pallas_call(
        functools.partial(_dsa_kernel, seq=seq, topk=topk),
        grid=(batch, nq),
        in_specs=[pl.BlockSpec((TQ, 512), lambda b, i: (b * nq + i, AB_QB // 512)),
                  pl.BlockSpec((TQ, 512), lambda b, i: (b * nq + i, AB_QI // 512)),
                  pl.BlockSpec((TQ, 128), lambda b, i: (b * nq + i, 1)),
                  pl.BlockSpec(memory_space=pl.ANY),
                  pl.BlockSpec(memory_space=pl.ANY),
                  pl.BlockSpec(memory_space=pl.ANY)],
        out_specs=pl.BlockSpec((TQ, 512), lambda b, i: (b * nq + i, 0)),
        out_shape=jax.ShapeDtypeStruct((n, 512), BF16),
        scratch_shapes=[pltpu.VMEM((seq, 512), BF16), pltpu.VMEM((seq, 512), BF16),
                        pltpu.VMEM((seq, 128), F32),
                        pltpu.VMEM((N_DELTA, DSA_HEADS, 128, 128), F32),
                        pltpu.VMEM((nk, TQ, TK), jnp.int32),
                        pltpu.VMEM((DSA_HEADS, TQ, 128), BF16), pltpu.VMEM((IDX_HEADS, TQ, 128), BF16),
                        pltpu.VMEM((TQ, TK), F32),
                        pltpu.VMEM((DSA_HEADS, TQ, 1), F32), pltpu.VMEM((DSA_HEADS, TQ, 1), F32),
                        pltpu.VMEM((DSA_HEADS, TQ, 128), F32),
                        pltpu.SemaphoreType.DMA((4,))],
        compiler_params=_cparams(("arbitrary", "arbitrary"), VMEM_LIMIT),
        name="dsa_attention",
    )(main, main, extra, main, extra, toeplitz)


DIL_TQ = 128


def _dil_kernel(q_ref, kp_ref, kc_ref, vp_ref, vc_ref, bm_ref, o_ref, ld_ref):
    i = pl.program_id(2)
    TQ = DIL_TQ
    col = lax.broadcasted_iota(jnp.int32, (TQ, 2 * TQ), 1)
    first = jnp.where((i == 0) & (col < TQ), MASKED, 0.0)
    for p in range(DIL_HEADS // 2):
        sl = slice(p * 128, (p + 1) * 128)
        q2 = q_ref[:, sl]
        kc = jnp.concatenate([kp_ref[:, sl], kc_ref[:, sl]], axis=0)
        vc = jnp.concatenate([vp_ref[:, sl], vc_ref[:, sl]], axis=0)
        outs, lds = [], []
        for half in range(2):
            h = 2 * p + half
            qm = jnp.where(_half_mask((TQ, 128), half), q2, jnp.zeros_like(q2))
            lg = _qk(qm, kc) + bm_ref[h] + first
            m = jnp.max(lg, axis=-1, keepdims=True)
            e = jnp.exp(lg - m)
            s = jnp.sum(e, axis=-1, keepdims=True)
            outs.append(jnp.dot(e.astype(BF16), vc, preferred_element_type=F32) / s)
            lds.append(jnp.broadcast_to(m + jnp.log(s), (TQ, 128)))
        lo = _half_mask((TQ, 128), 0)
        o_ref[:, sl] = jnp.where(lo, outs[0], outs[1])
        ld_ref[:, sl] = jnp.where(lo, lds[0], lds[1])


def _dilated_pattern(main, biasmask, dil, batch, seq):
    TQ = DIL_TQ
    n = batch * seq
    ld_seq = seq // dil
    nblk = ld_seq // TQ
    view = main.reshape(n // dil, dil * CD_MAIN)
    cpb = CD_MAIN // 512

    def qmap(b, r, i):
        return (b * nblk + i, r * cpb + CD_QD // 512)

    def kmap(off):
        def f(b, r, i):
            return (b * nblk + jnp.maximum(i - 1 + off, 0), r * cpb + CD_KD // 512)
        return f

    def vmap_(off):
        def f(b, r, i):
            return (b * nblk + jnp.maximum(i - 1 + off, 0), r * cpb + CD_VD // 512)
        return f

    o, ld = pl.pallas_call(
        _dil_kernel,
        grid=(batch, dil, nblk),
        in_specs=[pl.BlockSpec((TQ, 512), qmap),
                  pl.BlockSpec((TQ, 512), kmap(0)), pl.BlockSpec((TQ, 512), kmap(1)),
                  pl.BlockSpec((TQ, 512), vmap_(0)), pl.BlockSpec((TQ, 512), vmap_(1)),
                  pl.BlockSpec((DIL_HEADS, TQ, 2 * TQ), lambda b, r, i: (0, 0, 0))],
        out_specs=[pl.BlockSpec((TQ, 512), lambda b, r, i: (b * nblk + i, r)),
                   pl.BlockSpec((TQ, 512), lambda b, r, i: (b * nblk + i, r))],
        out_shape=[jax.ShapeDtypeStruct((n // dil, dil * 512), F32),
                   jax.ShapeDtypeStruct((n // dil, dil * 512), F32)],
        compiler_params=_cparams(("parallel", "parallel", "parallel"), VMEM_LIMIT),
        name="dilated_attention",
    )(view, view, view, view, view, biasmask)
    return o.reshape(n, 512), ld.reshape(n, 512)


def _dil_merge_kernel(o0, l0, o1, l1, o2, l2, out_ref):
    a, b, c = l0[...], l1[...], l2[...]
    mx = jnp.maximum(jnp.maximum(a, b), c)
    ea, eb, ec = jnp.exp(a - mx), jnp.exp(b - mx), jnp.exp(c - mx)
    tot = ea + eb + ec
    out_ref[...] = ((ea * o0[...] + eb * o1[...] + ec * o2[...]) / tot).astype(out_ref.dtype)


def _dilated(main, bvec, batch, seq):
    n = batch * seq
    parts = []
    a = jnp.arange(DIL_TQ)[:, None]
    c = jnp.arange(2 * DIL_TQ)[None, :]
    off = a - c + DIL_TQ
    inside = (off >= 0) & (off <= DIL_TQ)
    for window, dil in DIL_PATTERNS:
        assert window // dil == DIL_TQ
        bias = bvec[:, BVEC_SHIFT + dil * jnp.clip(off, 0, DIL_TQ)]
        biasmask = jnp.where(inside[None], bias, MASKED)
        parts.extend(_dilated_pattern(main, biasmask, dil, batch, seq))
    tm = 512
    spec = pl.BlockSpec((tm, 512), lambda i: (i, 0))
    return pl.pallas_call(
        _dil_merge_kernel,
        grid=(n // tm,),
        in_specs=[spec] * 6,
        out_specs=spec,
        out_shape=jax.ShapeDtypeStruct((n, 512), BF16),
        compiler_params=_cparams(("parallel",)),
        name="dilated_merge",
    )(*parts)


MOE_TM = 512
MOE_TF = 512


def _split3(a):
    hi = a.astype(BF16)
    r1 = a - hi.astype(F32)
    mid = r1.astype(BF16)
    lo = (r1 - mid.astype(F32)).astype(BF16)
    return hi, mid, lo


def _router_kernel(x_ref, w_ref, idx_ref, gate_ref):
    xh, xm, xl = _split3(x_ref[...])
    wh, wm, wl = _split3(w_ref[...])
    dot = functools.partial(jnp.dot, preferred_element_type=F32)
    logits = (dot(xh, wh) + (dot(xh, wm) + dot(xm, wh))
              + (dot(xh, wl) + dot(xm, wm) + dot(xl, wh)))
    lane = lax.broadcasted_iota(jnp.int32, logits.shape, 1)
    lg = jnp.where(lane < N_EXPERTS, logits, -jnp.inf)
    v1 = jnp.max(lg, axis=-1, keepdims=True)
    i1 = jnp.min(jnp.where(lg == v1, lane, LANES), axis=-1, keepdims=True)
    lg2 = jnp.where(lane == i1, -jnp.inf, lg)
    v2 = jnp.max(lg2, axis=-1, keepdims=True)
    i2 = jnp.min(jnp.where(lg2 == v2, lane, LANES), axis=-1, keepdims=True)
    e2 = jnp.exp(v2 - v1)
    den = 1.0 + e2
    idx_ref[...] = jnp.where(lane == 0, i1, jnp.where(lane == 1, i2, 0))
    gate_ref[...] = jnp.where(lane == 0, 1.0 / den, jnp.where(lane == 1, e2 / den, 0.0))


def _router(x, w_router_pad, tm=512):
    n, d = x.shape
    return pl.pallas_call(
        _router_kernel,
        grid=(n // tm,),
        in_specs=[pl.BlockSpec((tm, d), lambda i: (i, 0)),
                  pl.BlockSpec((d, LANES), lambda i: (0, 0))],
        out_specs=[pl.BlockSpec((tm, LANES), lambda i: (i, 0)),
                   pl.BlockSpec((tm, LANES), lambda i: (i, 0))],
        out_shape=[jax.ShapeDtypeStruct((n, LANES), jnp.int32),
                   jax.ShapeDtypeStruct((n, LANES), F32)],
        compiler_params=_cparams(("parallel",), VMEM_LIMIT),
        name="moe_router",
    )(x, w_router_pad)


GATHER_ROWS = 512


def _gather_kernel(idx_ref, src_hbm, out_hbm, sem):
    t = pl.program_id(0)
    base = t * GATHER_ROWS

    def row_copy(r):
        return pltpu.make_async_copy(src_hbm.at[pl.ds(idx_ref[base + r], 1)],
                                     out_hbm.at[pl.ds(base + r, 1)], sem)

    def issue(r, c):
        row_copy(r).start()
        return c

    def drain(r, c):
        row_copy(r).wait()
        return c

    lax.fori_loop(0, GATHER_ROWS, issue, 0)
    lax.fori_loop(0, GATHER_ROWS, drain, 0)


def _gather_rows(src, idx):
    m = idx.shape[0]
    return pl.pallas_call(
        _gather_kernel,
        grid_spec=pltpu.PrefetchScalarGridSpec(
            num_scalar_prefetch=1,
            grid=(m // GATHER_ROWS,),
            in_specs=[pl.BlockSpec(memory_space=pl.ANY)],
            out_specs=pl.BlockSpec(memory_space=pl.ANY),
            scratch_shapes=[pltpu.SemaphoreType.DMA(())],
        ),
        out_shape=jax.ShapeDtypeStruct((m, src.shape[1]), src.dtype),
        compiler_params=_cparams(("arbitrary",)),
        name="row_gather",
    )(idx, src)


def _moe_ffn_kernel(te_ref, nv_ref, xs_ref, w1_ref, w3_ref, w2_ref, gate_ref, o_ref, acc_ref):
    t = pl.program_id(0)
    f = pl.program_id(1)
    last = pl.num_programs(1) - 1
    live = t < nv_ref[0]

    @pl.when(f == 0)
    def _():
        acc_ref[...] = jnp.zeros_like(acc_ref)

    @pl.when(live)
    def _():
        xb = xs_ref[...].astype(BF16)
        a = jnp.dot(xb, w1_ref[...], preferred_element_type=F32)
        c = jnp.dot(xb, w3_ref[...], preferred_element_type=F32)
        h = (_silu(a) * c).astype(BF16)
        acc_ref[...] += jnp.dot(h, w2_ref[...], preferred_element_type=F32)

    @pl.when(f == last)
    def _():
        o_ref[...] = acc_ref[...] * gate_ref[...]


def _moe_ffn(xs, w1, w3, w2, gate_sorted, tile_expert, n_live):
    p_rows, d = xs.shape
    ff = w1.shape[2]
    tm, tf = MOE_TM, MOE_TF
    return pl.pallas_call(
        _moe_ffn_kernel,
        grid_spec=pltpu.PrefetchScalarGridSpec(
            num_scalar_prefetch=2,
            grid=(p_rows // tm, ff // tf),
            in_specs=[pl.BlockSpec((tm, d), lambda t, f, te, nv: (t, 0)),
                      pl.BlockSpec((None, d, tf), lambda t, f, te, nv: (te[t], 0, f)),
                      pl.BlockSpec((None, d, tf), lambda t, f, te, nv: (te[t], 0, f)),
                      pl.BlockSpec((None, tf, d), lambda t, f, te, nv: (te[t], f, 0)),
                      pl.BlockSpec((tm, 1), lambda t, f, te, nv: (t, 0))],
            out_specs=pl.BlockSpec((tm, d), lambda t, f, te, nv: (t, 0)),
            scratch_shapes=[pltpu.VMEM((tm, d), F32)],
        ),
        out_shape=jax.ShapeDtypeStruct((p_rows, d), F32),
        compiler_params=_cparams(("parallel", "arbitrary"), VMEM_LIMIT),
        name="moe_grouped_swiglu",
    )(tile_expert, n_live, xs, w1, w3, w2, gate_sorted)


def _combine_ln_kernel(y0_ref, y1_ref, x_ref, g_ref, b_ref, xf_ref, xb_ref):
    y = _layernorm_rows(ALPHA * x_ref[...] + (y0_ref[...] + y1_ref[...]), g_ref[...], b_ref[...])
    xf_ref[...] = y
    xb_ref[...] = y.astype(BF16)


def _combine_ln(yg, x, g, b, tm=512):
    n, d = x.shape
    nb = n // tm
    return pl.pallas_call(
        _combine_ln_kernel,
        grid=(nb,),
        in_specs=[pl.BlockSpec((tm, d), lambda i: (i, 0)),
                  pl.BlockSpec((tm, d), lambda i: (i + nb, 0)),
                  pl.BlockSpec((tm, d), lambda i: (i, 0)),
                  pl.BlockSpec((1, d), lambda i: (0, 0)),
                  pl.BlockSpec((1, d), lambda i: (0, 0))],
        out_specs=[pl.BlockSpec((tm, d), lambda i: (i, 0)),
                   pl.BlockSpec((tm, d), lambda i: (i, 0))],
        out_shape=[jax.ShapeDtypeStruct((n, d), F32), jax.ShapeDtypeStruct((n, d), BF16)],
        compiler_params=_cparams(("parallel",), VMEM_LIMIT),
        name="moe_combine_ln",
    )(yg, yg, x, g, b)


def _moe(x, w_router_pad, w1, w3, w2, g, b):
    n, d = x.shape
    tm = MOE_TM
    idx, gates = _router(x, w_router_pad)
    e_flat = idx[:, :TOP_K].T.reshape(-1)
    g_flat = gates[:, :TOP_K].T.reshape(-1)
    onehot = (e_flat[:, None] == jnp.arange(N_EXPERTS, dtype=jnp.int32)[None, :]).astype(jnp.int32)
    csum = jnp.cumsum(onehot, axis=0)
    rank = jnp.take_along_axis(csum, e_flat[:, None], axis=1)[:, 0] - 1
    counts = csum[-1]
    padded = ((counts + tm - 1) // tm) * tm
    ends = jnp.cumsum(padded)
    pos = (ends - padded)[e_flat] + rank
    p_rows = TOP_K * n + N_EXPERTS * tm
    token = jnp.tile(jnp.arange(n, dtype=jnp.int32), TOP_K)
    row_src = jnp.zeros((p_rows,), jnp.int32).at[pos].set(token)
    gate_sorted = jnp.zeros((p_rows,), F32).at[pos].set(g_flat).reshape(p_rows, 1)
    tile_start = jnp.arange(p_rows // tm, dtype=jnp.int32) * tm
    tile_expert = jnp.minimum(jnp.searchsorted(ends, tile_start, side="right"), N_EXPERTS - 1).astype(jnp.int32)
    n_live = (ends[-1] // tm).astype(jnp.int32).reshape(1)

    xs = _gather_rows(x, row_src)
    ys = _moe_ffn(xs, w1, w3, w2, gate_sorted, tile_expert, n_live)
    yg = _gather_rows(ys, pos.astype(jnp.int32))
    return _combine_ln(yg, x, g, b)


def _regroup_ab(w):
    main = jnp.concatenate([w[:, 0:1536], w[:, 1552:3600]], axis=1)
    ki = w[:, 3600:3664]
    extra = jnp.concatenate([ki, ki, w[:, 1536:1552], w[:, 3664:3672],
                             jnp.zeros((w.shape[0], 128 - GLA_GATE_RANK - IDX_HEADS), w.dtype)], axis=1)
    return main.astype(BF16), extra.astype(BF16)


def _regroup_cd(w):
    main = jnp.concatenate([w[:, 0:1536], w[:, 1544:3080]], axis=1)
    extra = jnp.concatenate([w[:, 1536:1544], jnp.zeros((w.shape[0], 128 - FOX_HEADS), w.dtype)], axis=1)
    return main.astype(BF16), extra.astype(BF16)


def kernel(x, ln_g, ln_b, rel_table, w_in_ab, w_gate_a, b_gate_a, g_norm_a, w_out_ab, w_in_cd, b_forget,
           w_out_cd, w1_dense, w3_dense, w2_dense, w_router, w1_moe, w3_moe, w2_moe):
    batch, seq, d = x.shape
    n = batch * seq
    xf = x.reshape(n, d)
    xb = xf.astype(BF16)

    bvec = _bias_vector(rel_table)
    ta = jnp.arange(128)
    dist = (jnp.arange(N_DELTA - 1)[:, None, None] * 128 + ta[None, :, None] - ta[None, None, :]) + BVEC_SHIFT
    near = bvec[:, dist]
    far = jnp.broadcast_to(bvec[:, BVEC_LEN - 1][:, None, None, None], (REL_HEADS, 1, 128, 128))
    toeplitz = jnp.concatenate([near, far], axis=1).transpose(1, 0, 2, 3)

    for layer in range(DEPTH):
        j = layer // 2
        g0, b0 = ln_g[layer, 0][None, :], ln_b[layer, 0][None, :]
        g1, b1 = ln_g[layer, 1][None, :], ln_b[layer, 1][None, :]
        if layer % 2 == 0:
            w_main, w_extra = _regroup_ab(w_in_ab[j])
            main = _matmul(xb, w_main, BF16, 512, 512)
            extra = _matmul(xb, w_extra, F32, 512, AB_EXTRA)
            oa = _gla(main, extra, w_gate_a[j], b_gate_a[j][None, :], g_norm_a[j][None, :], batch, seq)
            ob = _dsa(main, extra, toeplitz, batch, seq)
            o = jnp.concatenate([oa, ob], axis=1)
            xf, xb = _outproj_ln(o, w_out_ab[j].astype(BF16), xf, g0, b0)
            xf, xb = _swiglu_ln(xb, w1_dense[j].astype(BF16), w3_dense[j].astype(BF16),
                                w2_dense[j].astype(BF16), xf, g1, b1)
        else:
            w_main, w_extra = _regroup_cd(w_in_cd[j])
            main = _matmul(xb, w_main, BF16, 512, 512)
            extra = _matmul(xb, w_extra, F32, 512, CD_EXTRA)
            bf_row = jnp.concatenate([b_forget[j], jnp.zeros((128 - FOX_HEADS,), F32)])[None, :]
            fcum = _forget_cumsum(extra, bf_row, batch, seq)
            oc = _fox(main, fcum, batch, seq)
            od = _dilated(main, bvec, batch, seq)
            o = jnp.concatenate([oc, od], axis=1)
            xf, xb = _outproj_ln(o, w_out_cd[j].astype(BF16), xf, g0, b0)
            w_r = jnp.concatenate([w_router[j], jnp.zeros((d, LANES - N_EXPERTS), F32)], axis=1)
            xf, xb = _moe(xf, w_r, w1_moe[j].astype(BF16), w3_moe[j].astype(BF16),
                          w2_moe[j].astype(BF16), g1, b1)
    return xf.reshape(batch, seq, d)
```

```python
import functools
import math

import jax
import jax.numpy as jnp
from jax import lax
from jax.experimental import pallas as pl
from jax.experimental.pallas import tpu as pltpu

D_MODEL = 1024
DEPTH = 4
HEAD_DIM = 64
GLA_HEADS = 4
GLA_DK = 64
GLA_DV = 128
GLA_GATE_RANK = 16
GLA_GATE_TAU = 16.0
GLA_CHUNK = 64
DSA_HEADS = 8
IDX_HEADS = 8
IDX_DIM = 64
DSA_TOPK_MAX = 256
FOX_HEADS = 8
DIL_HEADS = 8
DIL_PATTERNS = ((128, 1), (512, 4), (2048, 16))
REL_BUCKETS = 32
REL_MAX_DIST = 2048
REL_HEADS = 8
D_FF = 2816
N_EXPERTS = 8
TOP_K = 2
D_FF_EXPERT = 3584
ALPHA = (2 * DEPTH) ** 0.25
LN_EPS = 1e-5

LANES = 128
MASKED = -1e30
M_INIT = -1e20
VMEM_LIMIT = 56 * 1024 * 1024

F32 = jnp.float32
BF16 = jnp.bfloat16

AB_QA, AB_KA, AB_VA, AB_RA, AB_QB, AB_KB, AB_VB, AB_QI = 0, 256, 512, 1024, 1536, 2048, 2560, 3072
AB_MAIN = 3584
AB_EXTRA = 256
EX_GA = 0
EX_WI = 16
CD_QC, CD_KC, CD_VC, CD_QD, CD_KD, CD_VD = 0, 512, 1024, 1536, 2048, 2560
CD_MAIN = 3072
CD_EXTRA = 128


def _cparams(sem, vmem=None):
    return pltpu.CompilerParams(dimension_semantics=sem, vmem_limit_bytes=vmem)


def _mm_kernel(x_ref, w_ref, o_ref):
    o_ref[...] = jnp.dot(x_ref[...], w_ref[...], preferred_element_type=F32).astype(o_ref.dtype)


def _matmul(x, w, out_dtype, tm, tn):
    n, k = x.shape
    m = w.shape[1]
    return pl.pallas_call(
        _mm_kernel,
        grid=(n // tm, m // tn),
        in_specs=[pl.BlockSpec((tm, k), lambda i, j: (i, 0)),
                  pl.BlockSpec((k, tn), lambda i, j: (0, j))],
        out_specs=pl.BlockSpec((tm, tn), lambda i, j: (i, j)),
        out_shape=jax.ShapeDtypeStruct((n, m), out_dtype),
        compiler_params=_cparams(("parallel", "parallel"), VMEM_LIMIT),
        name="proj_matmul",
    )(x, w)


def _layernorm_rows(z, g, b):
    mu = jnp.mean(z, axis=-1, keepdims=True)
    zc = z - mu
    var = jnp.mean(zc * zc, axis=-1, keepdims=True)
    return zc * lax.rsqrt(var + LN_EPS) * g + b


def _outproj_ln_kernel(o_ref, w_ref, x_ref, g_ref, b_ref, xf_ref, xb_ref):
    mix = jnp.dot(o_ref[...], w_ref[...], preferred_element_type=F32)
    y = _layernorm_rows(ALPHA * x_ref[...] + mix, g_ref[...], b_ref[...])
    xf_ref[...] = y
    xb_ref[...] = y.astype(BF16)


def _outproj_ln(o, w, x, g, b, tm=256):
    n, k = o.shape
    d = w.shape[1]
    return pl.pallas_call(
        _outproj_ln_kernel,
        grid=(n // tm,),
        in_specs=[pl.BlockSpec((tm, k), lambda i: (i, 0)),
                  pl.BlockSpec((k, d), lambda i: (0, 0)),
                  pl.BlockSpec((tm, d), lambda i: (i, 0)),
                  pl.BlockSpec((1, d), lambda i: (0, 0)),
                  pl.BlockSpec((1, d), lambda i: (0, 0))],
        out_specs=[pl.BlockSpec((tm, d), lambda i: (i, 0)),
                   pl.BlockSpec((tm, d), lambda i: (i, 0))],
        out_shape=[jax.ShapeDtypeStruct((n, d), F32), jax.ShapeDtypeStruct((n, d), BF16)],
        compiler_params=_cparams(("parallel",), VMEM_LIMIT),
        name="outproj_ln",
    )(o, w, x, g, b)


def _silu(a):
    return a / (1.0 + jnp.exp(-a))


def _swiglu_ln_kernel(xb_ref, w1_ref, w3_ref, w2_ref, x_ref, g_ref, b_ref, xf_ref, xbo_ref, acc_ref):
    f = pl.program_id(1)

    @pl.when(f == 0)
    def _():
        acc_ref[...] = jnp.zeros_like(acc_ref)

    xb = xb_ref[...]
    a = jnp.dot(xb, w1_ref[...], preferred_element_type=F32)
    c = jnp.dot(xb, w3_ref[...], preferred_element_type=F32)
    h = (_silu(a) * c).astype(BF16)
    acc_ref[...] += jnp.dot(h, w2_ref[...], preferred_element_type=F32)

    @pl.when(f == pl.num_programs(1) - 1)
    def _():
        y = _layernorm_rows(ALPHA * x_ref[...] + acc_ref[...], g_ref[...], b_ref[...])
        xf_ref[...] = y
        xbo_ref[...] = y.astype(BF16)


def _swiglu_ln(xb, w1, w3, w2, x, g, b, tm=512, tf=1408):
    n, d = xb.shape
    ff = w1.shape[1]
    return pl.pallas_call(
        _swiglu_ln_kernel,
        grid=(n // tm, ff // tf),
        in_specs=[pl.BlockSpec((tm, d), lambda i, f: (i, 0)),
                  pl.BlockSpec((d, tf), lambda i, f: (0, f)),
                  pl.BlockSpec((d, tf), lambda i, f: (0, f)),
                  pl.BlockSpec((tf, d), lambda i, f: (f, 0)),
                  pl.BlockSpec((tm, d), lambda i, f: (i, 0)),
                  pl.BlockSpec((1, d), lambda i, f: (0, 0)),
                  pl.BlockSpec((1, d), lambda i, f: (0, 0))],
        out_specs=[pl.BlockSpec((tm, d), lambda i, f: (i, 0)),
                   pl.BlockSpec((tm, d), lambda i, f: (i, 0))],
        out_shape=[jax.ShapeDtypeStruct((n, d), F32), jax.ShapeDtypeStruct((n, d), BF16)],
        scratch_shapes=[pltpu.VMEM((tm, d), F32)],
        compiler_params=_cparams(("parallel", "arbitrary"), VMEM_LIMIT),
        name="swiglu_ln",
    )(xb, w1, w3, w2, x, g, b)


BVEC_LEN = 2304
BVEC_SHIFT = 127


def _bvec_kernel(tab_ref, o_ref):
    n = lax.broadcasted_iota(jnp.int32, (REL_HEADS, BVEC_LEN), 1)
    d = jnp.maximum(n - BVEC_SHIFT, 0)
    max_exact = REL_BUCKETS // 2
    df = jnp.maximum(d, 1).astype(F32)
    large = max_exact + (jnp.log(df / max_exact) / math.log(REL_MAX_DIST / max_exact)
                         * (REL_BUCKETS - max_exact)).astype(jnp.int32)
    large = jnp.minimum(large, REL_BUCKETS - 1)
    bucket = jnp.where(d < max_exact, d, large)
    acc = jnp.zeros((REL_HEADS, BVEC_LEN), F32)
    for k in range(REL_BUCKETS):
        acc = jnp.where(bucket == k, tab_ref[:, k:k + 1], acc)
    o_ref[...] = acc


def _bias_vector(rel_table):
    return pl.pallas_call(
        _bvec_kernel,
        out_shape=jax.ShapeDtypeStruct((REL_HEADS, BVEC_LEN), F32),
        name="t5_bias_vector",
    )(rel_table.T)


GLA_TB = 256


def _log_sigmoid(z):
    return jnp.minimum(z, 0.0) - jnp.log(1.0 + jnp.exp(-jnp.abs(z)))


def _cumsum_rows(x):
    rows = x.shape[0]
    row = lax.broadcasted_iota(jnp.int32, x.shape, 0)
    sh = 1
    while sh < rows:
        x = x + jnp.where(row >= sh, pltpu.roll(x, sh, 0), 0.0)
        sh *= 2
    return x


def _gla_kernel(q_ref, k_ref, v_ref, r_ref, ga_ref, wg_ref, bg_ref, gn_ref, o_ref, s_ref, a_ref):
    c_id = pl.program_id(2)

    @pl.when(c_id == 0)
    def _():
        s_ref[...] = jnp.zeros_like(s_ref)

    C = GLA_CHUNK
    srow = lax.broadcasted_iota(jnp.int32, (2 * GLA_DK, 2 * GLA_DV), 0)
    scol = lax.broadcasted_iota(jnp.int32, (2 * GLA_DK, 2 * GLA_DV), 1)
    blockdiag = (srow < GLA_DK) == (scol < GLA_DV)
    trow = lax.broadcasted_iota(jnp.int32, (C, C), 0)
    tcol = lax.broadcasted_iota(jnp.int32, (C, C), 1)
    causal = tcol <= trow

    def chunk(c, carry):
        r0 = pl.multiple_of(c * C, C)
        q2 = q_ref[pl.ds(r0, C), :].astype(F32) * (GLA_DK ** -0.5)
        k2 = k_ref[pl.ds(r0, C), :].astype(F32)
        v2 = v_ref[pl.ds(r0, C), :]
        ga = ga_ref[pl.ds(r0, C), :][:, EX_GA:EX_GA + GLA_GATE_RANK]
        z = jnp.dot(ga.astype(BF16), wg_ref[...].astype(BF16), preferred_element_type=F32) + bg_ref[...]
        g = _log_sigmoid(z) / GLA_GATE_TAU
        G = _cumsum_rows(g)
        GT = G.T
        kT = k2.T
        qT = q2.T
        for t in range(C):
            dec = jnp.exp(GT[:, t:t + 1] - GT)
            prod = (qT[:, t:t + 1] * kT) * dec
            a_ref[0, t:t + 1, :] = jnp.sum(prod[:GLA_DK], axis=0, keepdims=True)
            a_ref[1, t:t + 1, :] = jnp.sum(prod[GLA_DK:], axis=0, keepdims=True)
        S = s_ref[...]
        o_inter = jnp.dot((q2 * jnp.exp(G)).astype(BF16), S.astype(BF16), preferred_element_type=F32)
        o_intra = []
        for h in range(2):
            A = jnp.where(causal, a_ref[h], 0.0)
            o_intra.append(jnp.dot(A.astype(BF16), v2[:, h * GLA_DV:(h + 1) * GLA_DV],
                                   preferred_element_type=F32))
        o2 = o_inter + jnp.concatenate(o_intra, axis=1)
        g_last = GT[:, C - 1:C]
        kd = kT * jnp.exp(g_last - GT)
        upd = jnp.dot(kd.astype(BF16), v2, preferred_element_type=F32)
        s_ref[...] = jnp.exp(g_last) * S + jnp.where(blockdiag, upd, 0.0)
        outs = []
        for h in range(2):
            of = o2[:, h * GLA_DV:(h + 1) * GLA_DV]
            of = of * lax.rsqrt(jnp.mean(of * of, axis=-1, keepdims=True) + LN_EPS) * gn_ref[...]
            rr = r_ref[pl.ds(r0, C), h * GLA_DV:(h + 1) * GLA_DV].astype(F32)
            outs.append(of * _silu(rr))
        o_ref[pl.ds(r0, C), :] = jnp.concatenate(outs, axis=1).astype(o_ref.dtype)
        return carry

    lax.fori_loop(0, GLA_TB // C, chunk, 0)


def _gla(main, extra, w_gate, b_gate, g_norm, batch, seq):
    nblk = seq // GLA_TB
    n = batch * seq

    def rows(b, hp, c):
        return b * nblk + c

    return pl.pallas_call(
        _gla_kernel,
        grid=(batch, 2, nblk),
        in_specs=[
            pl.BlockSpec((GLA_TB, 128), lambda b, hp, c: (rows(b, hp, c), AB_QA // 128 + hp)),
            pl.BlockSpec((GLA_TB, 128), lambda b, hp, c: (rows(b, hp, c), AB_KA // 128 + hp)),
            pl.BlockSpec((GLA_TB, 256), lambda b, hp, c: (rows(b, hp, c), AB_VA // 256 + hp)),
            pl.BlockSpec((GLA_TB, 256), lambda b, hp, c: (rows(b, hp, c), AB_RA // 256 + hp)),
            pl.BlockSpec((GLA_TB, 128), lambda b, hp, c: (rows(b, hp, c), 1)),
            pl.BlockSpec((GLA_GATE_RANK, 128), lambda b, hp, c: (0, hp)),
            pl.BlockSpec((1, 128), lambda b, hp, c: (0, hp)),
            pl.BlockSpec((1, GLA_DV), lambda b, hp, c: (0, 0)),
        ],
        out_specs=pl.BlockSpec((GLA_TB, 256), lambda b, hp, c: (rows(b, hp, c), hp)),
        out_shape=jax.ShapeDtypeStruct((n, GLA_HEADS * GLA_DV), BF16),
        scratch_shapes=[pltpu.VMEM((2 * GLA_DK, 2 * GLA_DV), F32),
                        pltpu.VMEM((2, GLA_CHUNK, GLA_CHUNK), F32)],
        compiler_params=_cparams(("parallel", "parallel", "arbitrary"), VMEM_LIMIT),
        name="gla",
    )(main, main, main, main, extra, w_gate, b_gate, g_norm)


ATT_TQ = 256
ATT_TK = 512


def _half_mask(shape, half):
    lane = lax.broadcasted_iota(jnp.int32, shape, len(shape) - 1)
    return (lane < HEAD_DIM) if half == 0 else (lane >= HEAD_DIM)


def _flash_step(s, v2, m, l, acc):
    m_new = jnp.maximum(m, jnp.max(s, axis=-1, keepdims=True))
    alpha = jnp.exp(m - m_new)
    p = jnp.exp(s - m_new)
    l = alpha * l + jnp.sum(p, axis=-1, keepdims=True)
    acc = alpha * acc + jnp.dot(p.astype(BF16), v2, preferred_element_type=F32)
    return m_new, l, acc


QK_SCALE = HEAD_DIM ** -0.5


def _qk(qm, k2):
    return lax.dot_general(qm, k2, (((1,), (1,)), ((), ())), preferred_element_type=F32)


def _load_kv(b, seq, srcs, dsts, sems):
    copies = []
    for n, (src, col, width) in enumerate(srcs):
        cp = pltpu.make_async_copy(src.at[pl.ds(b * seq, seq), pl.ds(col, width)], dsts[n], sems.at[n])
        cp.start()
        copies.append(cp)
    for cp in copies:
        cp.wait()


def _fcum_kernel(ex_ref, bf_ref, o_ref, carry_ref):
    i = pl.program_id(1)

    @pl.when(i == 0)
    def _():
        carry_ref[...] = jnp.zeros_like(carry_ref)

    lf = _log_sigmoid(ex_ref[...] + bf_ref[...])
    F = _cumsum_rows(lf) + carry_ref[...]
    o_ref[...] = F
    carry_ref[...] = F[F.shape[0] - 1:, :]


def _forget_cumsum(extra, b_forget_row, batch, seq, tb=256):
    nblk = seq // tb
    return pl.pallas_call(
        _fcum_kernel,
        grid=(batch, nblk),
        in_specs=[pl.BlockSpec((tb, 128), lambda b, i: (b * nblk + i, 0)),
                  pl.BlockSpec((1, 128), lambda b, i: (0, 0))],
        out_specs=pl.BlockSpec((tb, 128), lambda b, i: (b * nblk + i, 0)),
        out_shape=jax.ShapeDtypeStruct((batch * seq, 128), F32),
        scratch_shapes=[pltpu.VMEM((1, 128), F32)],
        compiler_params=_cparams(("parallel", "arbitrary")),
        name="forget_cumsum",
    )(extra, b_forget_row)


def _fox_kernel(q_ref, fq_ref, fk_ref, kv_hbm, o_ref, k_vm, v_vm, sems, *, seq):
    b = pl.program_id(0)
    i = pl.program_id(1)
    TQ, TK = ATT_TQ, ATT_TK

    @pl.when(i == 0)
    def _():
        _load_kv(b, seq, [(kv_hbm, CD_KC, 512), (kv_hbm, CD_VC, 512)], [k_vm, v_vm], sems)

    nj = (i * TQ + TQ - 1) // TK + 1
    row = i * TQ + lax.broadcasted_iota(jnp.int32, (TQ, TK), 0)
    col0 = lax.broadcasted_iota(jnp.int32, (TQ, TK), 1)
    for p in range(FOX_HEADS // 2):
        q2 = q_ref[:, p * 128:(p + 1) * 128]
        pair = []
        for half in range(2):
            h = 2 * p + half
            qm = jnp.where(_half_mask((TQ, 128), half), q2 * QK_SCALE, jnp.zeros_like(q2))
            fq = fq_ref[:, h:h + 1]

            def body(j, carry, diagonal=False, qm=qm, fq=fq, h=h, p=p):
                m, l, acc = carry
                k0 = pl.multiple_of(j * TK, TK)
                k2 = k_vm[pl.ds(k0, TK), p * 128:(p + 1) * 128]
                v2 = v_vm[pl.ds(k0, TK), p * 128:(p + 1) * 128]
                s = _qk(qm, k2) + (fq - fk_ref[h, pl.ds(j, 1), :])
                if diagonal:
                    s = jnp.where(col0 + j * TK <= row, s, MASKED)
                return _flash_step(s, v2, m, l, acc)

            init = (jnp.full((TQ, 1), M_INIT, F32), jnp.zeros((TQ, 1), F32), jnp.zeros((TQ, 128), F32))
            carry = lax.fori_loop(0, nj - 1, body, init)
            m, l, acc = body(nj - 1, carry, diagonal=True)
            pair.append(acc / l)
        o_ref[:, p * 128:(p + 1) * 128] = jnp.where(_half_mask((TQ, 128), 0), pair[0], pair[1]).astype(o_ref.dtype)


def _fox(main, fcum, batch, seq):
    TQ, TK = ATT_TQ, ATT_TK
    nq = seq // TQ
    nk = seq // TK
    n = batch * seq
    fk = fcum[:, :FOX_HEADS].reshape(batch, seq, FOX_HEADS).transpose(0, 2, 1).reshape(batch * FOX_HEADS, nk, TK)
    return pl.pallas_call(
        functools.partial(_fox_kernel, seq=seq),
        grid=(batch, nq),
        in_specs=[pl.BlockSpec((TQ, 512), lambda b, i: (b * nq + i, CD_QC // 512)),
                  pl.BlockSpec((TQ, 128), lambda b, i: (b * nq + i, 0)),
                  pl.BlockSpec((FOX_HEADS, nk, TK), lambda b, i: (b, 0, 0)),
                  pl.BlockSpec(memory_space=pl.ANY)],
        out_specs=pl.BlockSpec((TQ, 512), lambda b, i: (b * nq + i, 0)),
        out_shape=jax.ShapeDtypeStruct((n, 512), BF16),
        scratch_shapes=[pltpu.VMEM((seq, 512), BF16), pltpu.VMEM((seq, 512), BF16),
                        pltpu.SemaphoreType.DMA((2,))],
        compiler_params=_cparams(("arbitrary", "arbitrary"), VMEM_LIMIT),
        name="fox_attention",
    )(main, fcum, fk, main)


INT_MIN = -(2 ** 31)
N_DELTA = 18


def _sortable(x):
    b = pltpu.bitcast(x, jnp.int32)
    return b ^ ((b >> 31) & 0x7FFFFFFF)


def _fold_lanes(x):
    out = x[:, 0:LANES]
    for c in range(1, x.shape[1] // LANES):
        out = out + x[:, c * LANES:(c + 1) * LANES]
    return out


def _dsa_kernel(q_ref, qi_ref, ex_ref, kv_hbm, kid_hbm, tz_hbm, o_ref,
                k_vm, v_vm, kid_vm, tz_vm, key_vm, qm_vm, qim_vm, mask_vm, m_vm, l_vm, acc_vm, cut_vm, sems,
                *, seq, topk):
    b = pl.program_id(0)
    i = pl.program_id(1)
    TQ, TK = ATT_TQ, ATT_TK

    @pl.when(i == 0)
    def _():
        _load_kv(b, seq, [(kv_hbm, AB_KB, 512), (kv_hbm, AB_VB, 512), (kid_hbm, 0, 128)],
                 [k_vm, v_vm, kid_vm], sems)

    @pl.when((i == 0) & (b == 0))
    def _():
        cp = pltpu.make_async_copy(tz_hbm, tz_vm, sems.at[3])
        cp.start()
        cp.wait()

    nj = (i * TQ + TQ - 1) // TK + 1
    row = i * TQ + lax.broadcasted_iota(jnp.int32, (TQ, TK), 0)
    col0 = lax.broadcasted_iota(jnp.int32, (TQ, TK), 1)

    for p in range(4):
        q2 = q_ref[:, p * 128:(p + 1) * 128]
        qi2 = qi_ref[:, p * 128:(p + 1) * 128]
        for half in range(2):
            keep = _half_mask((TQ, 128), half)
            qm_vm[2 * p + half] = jnp.where(keep, q2 * QK_SCALE, jnp.zeros_like(q2))
            qim_vm[2 * p + half] = jnp.where(keep, qi2, jnp.zeros_like(qi2))

    def score_chunk(j, carry):
        k0 = pl.multiple_of(j * TK, TK)
        kd = kid_vm[pl.ds(k0, TK), :].astype(BF16)
        score = jnp.zeros((TQ, TK), F32)
        for h in range(IDX_HEADS):
            s = lax.dot_general(qim_vm[h], kd, (((1,), (1,)), ((), ())), preferred_element_type=F32)
            w = ex_ref[:, EX_WI + h:EX_WI + h + 1]
            score = score + w * jnp.maximum(s, 0.0)
        score = jnp.where(col0 + j * TK <= row, score, -jnp.inf)
        key_vm[j] = _sortable(score)
        return carry

    lax.fori_loop(0, nj, score_chunk, 0)

    RH = 128

    def count_rows(hit_fn):
        parts = []
        for r0 in range(0, TQ, RH):
            def body(j, acc, r0=r0):
                return acc + _fold_lanes(jnp.where(hit_fn(key_vm[j, r0:r0 + RH, :], j, r0), 1, 0))
            acc = lax.fori_loop(0, nj, body, jnp.zeros((RH, LANES), jnp.int32))
            parts.append(jnp.sum(acc, axis=-1, keepdims=True))
        return jnp.concatenate(parts, axis=0)

    def count_ge(c):
        return count_rows(lambda key, j, r0: key >= c[r0:r0 + RH])

    ans = jnp.where(count_ge(jnp.zeros((TQ, 1), jnp.int32)) >= topk, 0, INT_MIN).astype(jnp.int32)

    def bit_step(it, ans):
        cand = ans | (jnp.int32(1) << (30 - it))
        return jnp.where(count_ge(cand) >= topk, cand, ans)

    thr = lax.fori_loop(0, 31, bit_step, ans)

    cut_vm[...] = jnp.full((TQ, 1), seq, jnp.int32)
    n_ge = count_ge(thr)

    @pl.when(jnp.max(n_ge) > topk)
    def _():
        need = topk - count_ge(thr + 1)
        colh = lax.broadcasted_iota(jnp.int32, (RH, TK), 1)

        def count_eq_below(x):
            return count_rows(lambda key, j, r0: (key == thr[r0:r0 + RH]) & (colh + j * TK < x[r0:r0 + RH]))

        nbits = max(1, (seq - 1).bit_length())

        def cut_step(it, x):
            cand = x | (jnp.int32(1) << (nbits - 1 - it))
            return jnp.where(count_eq_below(cand) < need, cand, x)

        cut_vm[...] = lax.fori_loop(0, nbits, cut_step, jnp.zeros((TQ, 1), jnp.int32))

    cut = cut_vm[...]

    m_vm[...] = jnp.full(m_vm.shape, M_INIT, F32)
    l_vm[...] = jnp.zeros(l_vm.shape, F32)
    acc_vm[...] = jnp.zeros(acc_vm.shape, F32)

    def attend(j, carry):
        k0 = pl.multiple_of(j * TK, TK)
        col = col0 + j * TK
        key = key_vm[j]
        sel = ((key > thr) | ((key == thr) & (col <= cut))) & (col <= row)
        mask_vm[...] = jnp.where(sel, 0.0, MASKED)
        for h in range(DSA_HEADS):
            p = h // 2
            k2 = k_vm[pl.ds(k0, TK), p * 128:(p + 1) * 128]
            v2 = v_vm[pl.ds(k0, TK), p * 128:(p + 1) * 128]
            tiles = []
            for ra in range(TQ // 128):
                rowt = []
                for cb in range(TK // 128):
                    delta = jnp.clip(i * (TQ // 128) + ra - j * (TK // 128) - cb, 0, N_DELTA - 1)
                    rowt.append(tz_vm[delta, h])
                tiles.append(jnp.concatenate(rowt, axis=1))
            bias = jnp.concatenate(tiles, axis=0)
            s = _qk(qm_vm[h], k2) + bias + mask_vm[...]
            m, l, acc = _flash_step(s, v2, m_vm[h], l_vm[h], acc_vm[h])
            m_vm[h] = m
            l_vm[h] = l
            acc_vm[h] = acc
        return carry

    lax.fori_loop(0, nj, attend, 0)

    for p in range(4):
        o0 = acc_vm[2 * p] / l_vm[2 * p]
        o1 = acc_vm[2 * p + 1] / l_vm[2 * p + 1]
        o_ref[:, p * 128:(p + 1) * 128] = jnp.where(_half_mask((TQ, 128), 0), o0, o1).astype(o_ref.dtype)


def _dsa(main, extra, toeplitz, batch, seq):
    TQ, TK = ATT_TQ, ATT_TK
    nq = seq // TQ
    nk = seq // TK
    n = batch * seq
    topk = min(DSA_TOPK_MAX, seq // 4)
    return pl.pallas_call(
        functools.partial(_dsa_kernel, seq=seq, topk=topk),
        grid=(batch, nq),
        in_specs=[pl.BlockSpec((TQ, 512), lambda b, i: (b * nq + i, AB_QB // 512)),
                  pl.BlockSpec((TQ, 512), lambda b, i: (b * nq + i, AB_QI // 512)),
                  pl.BlockSpec((TQ, 128), lambda b, i: (b * nq + i, 1)),
                  pl.BlockSpec(memory_space=pl.ANY),
                  pl.BlockSpec(memory_space=pl.ANY),
                  pl.BlockSpec(memory_space=pl.ANY)],
        out_specs=pl.BlockSpec((TQ, 512), lambda b, i: (b * nq + i, 0)),
        out_shape=jax.ShapeDtypeStruct((n, 512), BF16),
        scratch_shapes=[pltpu.VMEM((seq, 512), BF16), pltpu.VMEM((seq, 512), BF16),
                        pltpu.VMEM((seq, 128), F32),
                        pltpu.VMEM((N_DELTA, DSA_HEADS, 128, 128), F32),
                        pltpu.VMEM((nk, TQ, TK), jnp.int32),
                        pltpu.VMEM((DSA_HEADS, TQ, 128), BF16), pltpu.VMEM((IDX_HEADS, TQ, 128), BF16),
                        pltpu.VMEM((TQ, TK), F32),
                        pltpu.VMEM((DSA_HEADS, TQ, 1), F32), pltpu.VMEM((DSA_HEADS, TQ, 1), F32),
                        pltpu.VMEM((DSA_HEADS, TQ, 128), F32),
                        pltpu.VMEM((TQ, 1), jnp.int32),
                        pltpu.SemaphoreType.DMA((4,))],
        compiler_params=_cparams(("arbitrary", "arbitrary"), VMEM_LIMIT),
        name="dsa_attention",
    )(main, main, extra, main, extra, toeplitz)


DIL_TQ = 128


def _dil_kernel(q_ref, kp_ref, kc_ref, vp_ref, vc_ref, bm_ref, o_ref, ld_ref):
    i = pl.program_id(2)
    TQ = DIL_TQ
    col = lax.broadcasted_iota(jnp.int32, (TQ, 2 * TQ), 1)
    first = jnp.where((i == 0) & (col < TQ), MASKED, 0.0)
    for p in range(DIL_HEADS // 2):
        sl = slice(p * 128, (p + 1) * 128)
        q2 = q_ref[:, sl]
        kc = jnp.concatenate([kp_ref[:, sl], kc_ref[:, sl]], axis=0)
        vc = jnp.concatenate([vp_ref[:, sl], vc_ref[:, sl]], axis=0)
        outs, lds = [], []
        for half in range(2):
            h = 2 * p + half
            qm = jnp.where(_half_mask((TQ, 128), half), q2 * QK_SCALE, jnp.zeros_like(q2))
            lg = _qk(qm, kc) + bm_ref[h] + first
            m = jnp.max(lg, axis=-1, keepdims=True)
            e = jnp.exp(lg - m)
            s = jnp.sum(e, axis=-1, keepdims=True)
            outs.append(jnp.dot(e.astype(BF16), vc, preferred_element_type=F32) / s)
            lds.append(jnp.broadcast_to(m + jnp.log(s), (TQ, 128)))
        lo = _half_mask((TQ, 128), 0)
        o_ref[:, sl] = jnp.where(lo, outs[0], outs[1])
        ld_ref[:, sl] = jnp.where(lo, lds[0], lds[1])


def _dilated_pattern(main, biasmask, dil, batch, seq):
    TQ = DIL_TQ
    n = batch * seq
    ld_seq = seq // dil
    nblk = ld_seq // TQ
    view = main.reshape(n // dil, dil * CD_MAIN)
    cpb = CD_MAIN // 512

    def qmap(b, r, i):
        return (b * nblk + i, r * cpb + CD_QD // 512)

    def kmap(off):
        def f(b, r, i):
            return (b * nblk + jnp.maximum(i - 1 + off, 0), r * cpb + CD_KD // 512)
        return f

    def vmap_(off):
        def f(b, r, i):
            return (b * nblk + jnp.maximum(i - 1 + off, 0), r * cpb + CD_VD // 512)
        return f

    o, ld = pl.pallas_call(
        _dil_kernel,
        grid=(batch, dil, nblk),
        in_specs=[pl.BlockSpec((TQ, 512), qmap),
                  pl.BlockSpec((TQ, 512), kmap(0)), pl.BlockSpec((TQ, 512), kmap(1)),
                  pl.BlockSpec((TQ, 512), vmap_(0)), pl.BlockSpec((TQ, 512), vmap_(1)),
                  pl.BlockSpec((DIL_HEADS, TQ, 2 * TQ), lambda b, r, i: (0, 0, 0))],
        out_specs=[pl.BlockSpec((TQ, 512), lambda b, r, i: (b * nblk + i, r)),
                   pl.BlockSpec((TQ, 512), lambda b, r, i: (b * nblk + i, r))],
        out_shape=[jax.ShapeDtypeStruct((n // dil, dil * 512), F32),
                   jax.ShapeDtypeStruct((n // dil, dil * 512), F32)],
        compiler_params=_cparams(("parallel", "parallel", "parallel"), VMEM_LIMIT),
        name="dilated_attention",
    )(view, view, view, view, view, biasmask)
    return o.reshape(n, 512), ld.reshape(n, 512)


def _dil_merge_kernel(o0, l0, o1, l1, o2, l2, out_ref):
    a, b, c = l0[...], l1[...], l2[...]
    mx = jnp.maximum(jnp.maximum(a, b), c)
    ea, eb, ec = jnp.exp(a - mx), jnp.exp(b - mx), jnp.exp(c - mx)
    tot = ea + eb + ec
    out_ref[...] = ((ea * o0[...] + eb * o1[...] + ec * o2[...]) / tot).astype(out_ref.dtype)


def _dilated(main, bvec, batch, seq):
    n = batch * seq
    parts = []
    a = jnp.arange(DIL_TQ)[:, None]
    c = jnp.arange(2 * DIL_TQ)[None, :]
    off = a - c + DIL_TQ
    inside = (off >= 0) & (off <= DIL_TQ)
    for window, dil in DIL_PATTERNS:
        assert window // dil == DIL_TQ
        bias = bvec[:, BVEC_SHIFT + dil * jnp.clip(off, 0, DIL_TQ)]
        biasmask = jnp.where(inside[None], bias, MASKED)
        parts.extend(_dilated_pattern(main, biasmask, dil, batch, seq))
    tm = 512
    spec = pl.BlockSpec((tm, 512), lambda i: (i, 0))
    return pl.pallas_call(
        _dil_merge_kernel,
        grid=(n // tm,),
        in_specs=[spec] * 6,
        out_specs=spec,
        out_shape=jax.ShapeDtypeStruct((n, 512), BF16),
        compiler_params=_cparams(("parallel",)),
        name="dilated_merge",
    )(*parts)


MOE_TM = 512
MOE_TF = 512


def _split3(a):
    hi = a.astype(BF16)
    r1 = a - hi.astype(F32)
    mid = r1.astype(BF16)
    lo = (r1 - mid.astype(F32)).astype(BF16)
    return hi, mid, lo


def _router_kernel(x_ref, w_ref, idx_ref, gate_ref):
    xh, xm, xl = _split3(x_ref[...])
    wh, wm, wl = _split3(w_ref[...])
    dot = functools.partial(jnp.dot, preferred_element_type=F32)
    logits = (dot(xh, wh) + (dot(xh, wm) + dot(xm, wh))
              + (dot(xh, wl) + dot(xm, wm) + dot(xl, wh)))
    lane = lax.broadcasted_iota(jnp.int32, logits.shape, 1)
    lg = jnp.where(lane < N_EXPERTS, logits, -jnp.inf)
    v1 = jnp.max(lg, axis=-1, keepdims=True)
    i1 = jnp.min(jnp.where(lg == v1, lane, LANES), axis=-1, keepdims=True)
    lg2 = jnp.where(lane == i1, -jnp.inf, lg)
    v2 = jnp.max(lg2, axis=-1, keepdims=True)
    i2 = jnp.min(jnp.where(lg2 == v2, lane, LANES), axis=-1, keepdims=True)
    e2 = jnp.exp(v2 - v1)
    den = 1.0 + e2
    idx_ref[...] = jnp.where(lane == 0, i1, jnp.where(lane == 1, i2, 0))
    gate_ref[...] = jnp.where(lane == 0, 1.0 / den, jnp.where(lane == 1, e2 / den, 0.0))


def _router(x, w_router_pad, tm=512):
    n, d = x.shape
    return pl.pallas_call(
        _router_kernel,
        grid=(n // tm,),
        in_specs=[pl.BlockSpec((tm, d), lambda i: (i, 0)),
                  pl.BlockSpec((d, LANES), lambda i: (0, 0))],
        out_specs=[pl.BlockSpec((tm, LANES), lambda i: (i, 0)),
                   pl.BlockSpec((tm, LANES), lambda i: (i, 0))],
        out_shape=[jax.ShapeDtypeStruct((n, LANES), jnp.int32),
                   jax.ShapeDtypeStruct((n, LANES), F32)],
        compiler_params=_cparams(("parallel",), VMEM_LIMIT),
        name="moe_router",
    )(x, w_router_pad)


MOE_TC = 512
MOE_WIN = 640


def _moe_ffn_kernel(te_ref, nv_ref, clo_ref, chi_ref, src_ref, x_hbm, w1_ref, w3_ref, w2_ref, gate_ref,
                    hi_ref, lo_ref, acc_ref, xs_ref, xbuf, sems):
    t = pl.program_id(0)
    f = pl.program_id(1)
    last = pl.num_programs(1) - 1
    live = t < nv_ref[0]
    tm = acc_ref.shape[0]

    def fetch(c, slot):
        return pltpu.make_async_copy(x_hbm.at[pl.ds(c * MOE_TC, MOE_TC)], xbuf.at[slot], sems.at[slot])

    @pl.when(live & (f == 0))
    def _():
        clo = clo_ref[t]
        chi = chi_ref[t]
        acc_ref[...] = jnp.zeros_like(acc_ref)
        fetch(clo, 0).start()
        src = src_ref[...]
        lane = lax.broadcasted_iota(jnp.int32, (tm, MOE_TC), 1)

        def body(c, carry):
            slot = (c - clo) & 1

            @pl.when(c < chi)
            def _():
                fetch(c + 1, 1 - slot).start()

            fetch(c, slot).wait()
            onehot = jnp.where(src == lane + c * MOE_TC, 1.0, 0.0).astype(BF16)
            acc_ref[...] += jnp.dot(onehot, xbuf[slot], preferred_element_type=F32)
            return carry

        lax.fori_loop(clo, chi + 1, body, 0)
        xs_ref[...] = acc_ref[...].astype(BF16)

    @pl.when(f == 0)
    def _():
        acc_ref[...] = jnp.zeros_like(acc_ref)

    @pl.when(live)
    def _():
        xb = xs_ref[...]
        a = jnp.dot(xb, w1_ref[...], preferred_element_type=F32)
        c = jnp.dot(xb, w3_ref[...], preferred_element_type=F32)
        h = (_silu(a) * c).astype(BF16)
        acc_ref[...] += jnp.dot(h, w2_ref[...], preferred_element_type=F32)

    @pl.when(f == last)
    def _():
        y = acc_ref[...] * gate_ref[...]
        hi = y.astype(BF16)
        hi_ref[...] = hi
        lo_ref[...] = (y - hi.astype(F32)).astype(BF16)


def _moe_ffn(xb, row_src, w1, w3, w2, gate_sorted, tile_expert, n_live, chunk_lo, chunk_hi):
    p_rows = row_src.shape[0]
    d = xb.shape[1]
    ff = w1.shape[2]
    tm, tf = MOE_TM, MOE_TF

    def rows(t, f, te, nv, clo, chi):
        return (t, 0)

    return pl.pallas_call(
        _moe_ffn_kernel,
        grid_spec=pltpu.PrefetchScalarGridSpec(
            num_scalar_prefetch=4,
            grid=(p_rows // tm, ff // tf),
            in_specs=[pl.BlockSpec((tm, 1), rows),
                      pl.BlockSpec(memory_space=pl.ANY),
                      pl.BlockSpec((None, d, tf), lambda t, f, te, nv, clo, chi: (te[t], 0, f)),
                      pl.BlockSpec((None, d, tf), lambda t, f, te, nv, clo, chi: (te[t], 0, f)),
                      pl.BlockSpec((None, tf, d), lambda t, f, te, nv, clo, chi: (te[t], f, 0)),
                      pl.BlockSpec((tm, 1), rows)],
            out_specs=[pl.BlockSpec((tm, d), rows), pl.BlockSpec((tm, d), rows)],
            scratch_shapes=[pltpu.VMEM((tm, d), F32), pltpu.VMEM((tm, d), BF16),
                            pltpu.VMEM((2, MOE_TC, d), BF16), pltpu.SemaphoreType.DMA((2,))],
        ),
        out_shape=[jax.ShapeDtypeStruct((p_rows, d), BF16), jax.ShapeDtypeStruct((p_rows, d), BF16)],
        compiler_params=_cparams(("arbitrary", "arbitrary"), VMEM_LIMIT),
        name="moe_grouped_swiglu",
    )(tile_expert, n_live, chunk_lo, chunk_hi, row_src, xb, w1, w3, w2, gate_sorted)


def _combine_ln_kernel(win_ref, pos_ref, eid_ref, hi_hbm, lo_hbm, x_ref, g_ref, b_ref, xf_ref, xb_ref,
                       acc_ref, hbuf, lbuf, sems):
    blk = pl.program_id(0)
    tm = acc_ref.shape[0]

    def fetch(e, slot):
        w = pl.multiple_of(win_ref[blk * N_EXPERTS + e], 16)
        return (pltpu.make_async_copy(hi_hbm.at[pl.ds(w, MOE_WIN)], hbuf.at[slot], sems.at[0, slot]),
                pltpu.make_async_copy(lo_hbm.at[pl.ds(w, MOE_WIN)], lbuf.at[slot], sems.at[1, slot]))

    for cp in fetch(0, 0):
        cp.start()
    pos0 = pos_ref[:, 0:1]
    pos1 = pos_ref[:, 1:2]
    lane = lax.broadcasted_iota(jnp.int32, (tm, MOE_WIN), 1)
    acc_ref[...] = ALPHA * x_ref[...]
    for e in range(N_EXPERTS):
        slot = e & 1
        if e + 1 < N_EXPERTS:
            for cp in fetch(e + 1, 1 - slot):
                cp.start()
        for cp in fetch(e, slot):
            cp.wait()
        w = win_ref[blk * N_EXPERTS + e]
        rel0 = jnp.where(eid_ref[:, 0:1] == e, pos0 - w, -1)
        rel1 = jnp.where(eid_ref[:, 1:2] == e, pos1 - w, -1)
        onehot = jnp.where((lane == rel0) | (lane == rel1), 1.0, 0.0).astype(BF16)
        acc_ref[...] += (jnp.dot(onehot, hbuf[slot], preferred_element_type=F32)
                         + jnp.dot(onehot, lbuf[slot], preferred_element_type=F32))
    y = _layernorm_rows(acc_ref[...], g_ref[...], b_ref[...])
    xf_ref[...] = y
    xb_ref[...] = y.astype(BF16)


def _combine_ln(win, pos2, eid2, ys_hi, ys_lo, x, g, b):
    n, d = x.shape
    tm = MOE_TC
    return pl.pallas_call(
        _combine_ln_kernel,
        grid_spec=pltpu.PrefetchScalarGridSpec(
            num_scalar_prefetch=1,
            grid=(n // tm,),
            in_specs=[pl.BlockSpec((tm, TOP_K), lambda i, w: (i, 0)),
                      pl.BlockSpec((tm, TOP_K), lambda i, w: (i, 0)),
                      pl.BlockSpec(memory_space=pl.ANY),
                      pl.BlockSpec(memory_space=pl.ANY),
                      pl.BlockSpec((tm, d), lambda i, w: (i, 0)),
                      pl.BlockSpec((1, d), lambda i, w: (0, 0)),
                      pl.BlockSpec((1, d), lambda i, w: (0, 0))],
            out_specs=[pl.BlockSpec((tm, d), lambda i, w: (i, 0)),
                       pl.BlockSpec((tm, d), lambda i, w: (i, 0))],
            scratch_shapes=[pltpu.VMEM((tm, d), F32),
                            pltpu.VMEM((2, MOE_WIN, d), BF16), pltpu.VMEM((2, MOE_WIN, d), BF16),
                            pltpu.SemaphoreType.DMA((2, 2))],
        ),
        out_shape=[jax.ShapeDtypeStruct((n, d), F32), jax.ShapeDtypeStruct((n, d), BF16)],
        compiler_params=_cparams(("arbitrary",), VMEM_LIMIT),
        name="moe_combine_ln",
    )(win, pos2, eid2, ys_hi, ys_lo, x, g, b)


def _moe(x, xb, w_router_pad, w1, w3, w2, g, b):
    n, d = x.shape
    tm = MOE_TM
    idx, gates = _router(x, w_router_pad)
    e_flat = idx[:, :TOP_K].reshape(-1)
    g_flat = gates[:, :TOP_K].reshape(-1)
    onehot = (e_flat[:, None] == jnp.arange(N_EXPERTS, dtype=jnp.int32)[None, :]).astype(jnp.int32)
    csum = jnp.cumsum(onehot, axis=0)
    rank = jnp.take_along_axis(csum, e_flat[:, None], axis=1)[:, 0] - 1
    counts = csum[-1]
    padded = ((counts + tm - 1) // tm) * tm
    ends = jnp.cumsum(padded)
    pos = ((ends - padded)[e_flat] + rank).astype(jnp.int32)
    tiles = (TOP_K * n) // tm + N_EXPERTS + 2
    p_rows = tiles * tm
    token = jnp.arange(TOP_K * n, dtype=jnp.int32) // TOP_K
    row_src = jnp.full((p_rows,), -1, jnp.int32).at[pos].set(token)
    gate_sorted = jnp.zeros((p_rows,), F32).at[pos].set(g_flat).reshape(p_rows, 1)
    tile_start = jnp.arange(tiles, dtype=jnp.int32) * tm
    tile_expert = jnp.minimum(jnp.searchsorted(ends, tile_start, side="right"), N_EXPERTS - 1).astype(jnp.int32)
    n_live = (ends[-1] // tm).astype(jnp.int32).reshape(1)
    per_tile = row_src.reshape(tiles, tm)
    chunk_lo = (jnp.min(jnp.where(per_tile >= 0, per_tile, n - 1), axis=1) // MOE_TC).astype(jnp.int32)
    chunk_hi = (jnp.max(per_tile, axis=1) // MOE_TC).astype(jnp.int32)
    big = jnp.int32(p_rows)
    wkey = (token // MOE_TC) * N_EXPERTS + e_flat
    win = jnp.full(((n // MOE_TC) * N_EXPERTS,), big, jnp.int32).at[wkey].min(pos)
    win = (jnp.where(win == big, 0, win) // 16) * 16

    ys_hi, ys_lo = _moe_ffn(xb, row_src.reshape(p_rows, 1), w1, w3, w2, gate_sorted, tile_expert, n_live,
                            chunk_lo, chunk_hi)
    return _combine_ln(win, pos.reshape(n, TOP_K), idx[:, :TOP_K], ys_hi, ys_lo, x, g, b)


def _regroup_ab(w):
    main = jnp.concatenate([w[:, 0:1536], w[:, 1552:3600]], axis=1)
    ki = w[:, 3600:3664]
    extra = jnp.concatenate([ki, ki, w[:, 1536:1552], w[:, 3664:3672],
                             jnp.zeros((w.shape[0], 128 - GLA_GATE_RANK - IDX_HEADS), w.dtype)], axis=1)
    return main.astype(BF16), extra.astype(BF16)


def _regroup_cd(w):
    main = jnp.concatenate([w[:, 0:1536], w[:, 1544:3080]], axis=1)
    extra = jnp.concatenate([w[:, 1536:1544], jnp.zeros((w.shape[0], 128 - FOX_HEADS), w.dtype)], axis=1)
    return main.astype(BF16), extra.astype(BF16)


def kernel(x, ln_g, ln_b, rel_table, w_in_ab, w_gate_a, b_gate_a, g_norm_a, w_out_ab, w_in_cd, b_forget,
           w_out_cd, w1_dense, w3_dense, w2_dense, w_router, w1_moe, w3_moe, w2_moe):
    batch, seq, d = x.shape
    n = batch * seq
    xf = x.reshape(n, d)
    xb = xf.astype(BF16)

    bvec = _bias_vector(rel_table)
    ta = jnp.arange(128)
    dist = (jnp.arange(N_DELTA - 1)[:, None, None] * 128 + ta[None, :, None] - ta[None, None, :]) + BVEC_SHIFT
    near = bvec[:, dist]
    far = jnp.broadcast_to(bvec[:, BVEC_LEN - 1][:, None, None, None], (REL_HEADS, 1, 128, 128))
    toeplitz = jnp.concatenate([near, far], axis=1).transpose(1, 0, 2, 3)

    for layer in range(DEPTH):
        j = layer // 2
        g0, b0 = ln_g[layer, 0][None, :], ln_b[layer, 0][None, :]
        g1, b1 = ln_g[layer, 1][None, :], ln_b[layer, 1][None, :]
        if layer % 2 == 0:
            w_main, w_extra = _regroup_ab(w_in_ab[j])
            main = _matmul(xb, w_main, BF16, 512, 512)
            extra = _matmul(xb, w_extra, F32, 512, AB_EXTRA)
            oa = _gla(main, extra, w_gate_a[j], b_gate_a[j][None, :], g_norm_a[j][None, :], batch, seq)
            ob = _dsa(main, extra, toeplitz, batch, seq)
            o = jnp.concatenate([oa, ob], axis=1)
            xf, xb = _outproj_ln(o, w_out_ab[j].astype(BF16), xf, g0, b0)
            xf, xb = _swiglu_ln(xb, w1_dense[j].astype(BF16), w3_dense[j].astype(BF16),
                                w2_dense[j].astype(BF16), xf, g1, b1)
        else:
            w_main, w_extra = _regroup_cd(w_in_cd[j])
            main = _matmul(xb, w_main, BF16, 512, 512)
            extra = _matmul(xb, w_extra, F32, 512, CD_EXTRA)
            bf_row = jnp.concatenate([b_forget[j], jnp.zeros((128 - FOX_HEADS,), F32)])[None, :]
            fcum = _forget_cumsum(extra, bf_row, batch, seq)
            oc = _fox(main, fcum, batch, seq)
            od = _dilated(main, bvec, batch, seq)
            o = jnp.concatenate([oc, od], axis=1)
            xf, xb = _outproj_ln(o, w_out_cd[j].astype(BF16), xf, g0, b0)
            w_r = jnp.concatenate([w_router[j], jnp.zeros((d, LANES - N_EXPERTS), F32)], axis=1)
            xf, xb = _moe(xf, xb, w_r, w1_moe[j].astype(BF16), w3_moe[j].astype(BF16),
                          w2_moe[j].astype(BF16), g1, b1)
    return xf.reshape(batch, seq, d)
```

```python
import functools
import math

import jax
import jax.numpy as jnp
from jax import lax
from jax.experimental import pallas as pl
from jax.experimental.pallas import tpu as pltpu

D_MODEL = 1024
DEPTH = 4
HEAD_DIM = 64
GLA_HEADS = 4
GLA_DK = 64
GLA_DV = 128
GLA_GATE_RANK = 16
GLA_GATE_TAU = 16.0
GLA_CHUNK = 64
DSA_HEADS = 8
IDX_HEADS = 8
IDX_DIM = 64
DSA_TOPK_MAX = 256
FOX_HEADS = 8
DIL_HEADS = 8
DIL_PATTERNS = ((128, 1), (512, 4), (2048, 16))
REL_BUCKETS = 32
REL_MAX_DIST = 2048
REL_HEADS = 8
D_FF = 2816
N_EXPERTS = 8
TOP_K = 2
D_FF_EXPERT = 3584
ALPHA = (2 * DEPTH) ** 0.25
LN_EPS = 1e-5

LANES = 128
MASKED = -1e30
M_INIT = -1e20
VMEM_LIMIT = 56 * 1024 * 1024

F32 = jnp.float32
BF16 = jnp.bfloat16

AB_QA, AB_KA, AB_VA, AB_RA, AB_QB, AB_KB, AB_VB, AB_QI = 0, 256, 512, 1024, 1536, 2048, 2560, 3072
AB_MAIN = 3584
AB_EXTRA = 256
EX_GA = 0
EX_WI = 16
CD_QC, CD_KC, CD_VC, CD_QD, CD_KD, CD_VD = 0, 512, 1024, 1536, 2048, 2560
CD_MAIN = 3072
CD_EXTRA = 128


def _cparams(sem, vmem=None):
    return pltpu.CompilerParams(dimension_semantics=sem, vmem_limit_bytes=vmem)


def _mm_kernel(x_ref, w_ref, o_ref):
    o_ref[...] = jnp.dot(x_ref[...], w_ref[...], preferred_element_type=F32).astype(o_ref.dtype)


def _matmul(x, w, out_dtype, tm, tn):
    n, k = x.shape
    m = w.shape[1]
    return pl.pallas_call(
        _mm_kernel,
        grid=(n // tm, m // tn),
        in_specs=[pl.BlockSpec((tm, k), lambda i, j: (i, 0)),
                  pl.BlockSpec((k, tn), lambda i, j: (0, j))],
        out_specs=pl.BlockSpec((tm, tn), lambda i, j: (i, j)),
        out_shape=jax.ShapeDtypeStruct((n, m), out_dtype),
        compiler_params=_cparams(("parallel", "parallel"), VMEM_LIMIT),
        name="proj_matmul",
    )(x, w)


def _layernorm_rows(z, g, b):
    mu = jnp.mean(z, axis=-1, keepdims=True)
    zc = z - mu
    var = jnp.mean(zc * zc, axis=-1, keepdims=True)
    return zc * lax.rsqrt(var + LN_EPS) * g + b


def _outproj_ln_kernel(o_ref, w_ref, x_ref, g_ref, b_ref, xf_ref, xb_ref):
    mix = jnp.dot(o_ref[...], w_ref[...], preferred_element_type=F32)
    y = _layernorm_rows(ALPHA * x_ref[...] + mix, g_ref[...], b_ref[...])
    xf_ref[...] = y
    xb_ref[...] = y.astype(BF16)


def _outproj_ln(o, w, x, g, b, tm=256):
    n, k = o.shape
    d = w.shape[1]
    return pl.pallas_call(
        _outproj_ln_kernel,
        grid=(n // tm,),
        in_specs=[pl.BlockSpec((tm, k), lambda i: (i, 0)),
                  pl.BlockSpec((k, d), lambda i: (0, 0)),
                  pl.BlockSpec((tm, d), lambda i: (i, 0)),
                  pl.BlockSpec((1, d), lambda i: (0, 0)),
                  pl.BlockSpec((1, d), lambda i: (0, 0))],
        out_specs=[pl.BlockSpec((tm, d), lambda i: (i, 0)),
                   pl.BlockSpec((tm, d), lambda i: (i, 0))],
        out_shape=[jax.ShapeDtypeStruct((n, d), F32), jax.ShapeDtypeStruct((n, d), BF16)],
        compiler_params=_cparams(("parallel",), VMEM_LIMIT),
        name="outproj_ln",
    )(o, w, x, g, b)


def _silu(a):
    return a / (1.0 + jnp.exp(-a))


def _swiglu_ln_kernel(xb_ref, w1_ref, w3_ref, w2_ref, x_ref, g_ref, b_ref, xf_ref, xbo_ref, acc_ref):
    f = pl.program_id(1)

    @pl.when(f == 0)
    def _():
        acc_ref[...] = jnp.zeros_like(acc_ref)

    xb = xb_ref[...]
    a = jnp.dot(xb, w1_ref[...], preferred_element_type=F32)
    c = jnp.dot(xb, w3_ref[...], preferred_element_type=F32)
    h = (_silu(a) * c).astype(BF16)
    acc_ref[...] += jnp.dot(h, w2_ref[...], preferred_element_type=F32)

    @pl.when(f == pl.num_programs(1) - 1)
    def _():
        y = _layernorm_rows(ALPHA * x_ref[...] + acc_ref[...], g_ref[...], b_ref[...])
        xf_ref[...] = y
        xbo_ref[...] = y.astype(BF16)


def _swiglu_ln(xb, w1, w3, w2, x, g, b, tm=512, tf=1408):
    n, d = xb.shape
    ff = w1.shape[1]
    return pl.pallas_call(
        _swiglu_ln_kernel,
        grid=(n // tm, ff // tf),
        in_specs=[pl.BlockSpec((tm, d), lambda i, f: (i, 0)),
                  pl.BlockSpec((d, tf), lambda i, f: (0, f)),
                  pl.BlockSpec((d, tf), lambda i, f: (0, f)),
                  pl.BlockSpec((tf, d), lambda i, f: (f, 0)),
                  pl.BlockSpec((tm, d), lambda i, f: (i, 0)),
                  pl.BlockSpec((1, d), lambda i, f: (0, 0)),
                  pl.BlockSpec((1, d), lambda i, f: (0, 0))],
        out_specs=[pl.BlockSpec((tm, d), lambda i, f: (i, 0)),
                   pl.BlockSpec((tm, d), lambda i, f: (i, 0))],
        out_shape=[jax.ShapeDtypeStruct((n, d), F32), jax.ShapeDtypeStruct((n, d), BF16)],
        scratch_shapes=[pltpu.VMEM((tm, d), F32)],
        compiler_params=_cparams(("parallel", "arbitrary"), VMEM_LIMIT),
        name="swiglu_ln",
    )(xb, w1, w3, w2, x, g, b)


BVEC_LEN = 2304
BVEC_SHIFT = 127


def _bvec_kernel(tab_ref, o_ref):
    n = lax.broadcasted_iota(jnp.int32, (REL_HEADS, BVEC_LEN), 1)
    d = jnp.maximum(n - BVEC_SHIFT, 0)
    max_exact = REL_BUCKETS // 2
    df = jnp.maximum(d, 1).astype(F32)
    large = max_exact + (jnp.log(df / max_exact) / math.log(REL_MAX_DIST / max_exact)
                         * (REL_BUCKETS - max_exact)).astype(jnp.int32)
    large = jnp.minimum(large, REL_BUCKETS - 1)
    bucket = jnp.where(d < max_exact, d, large)
    acc = jnp.zeros((REL_HEADS, BVEC_LEN), F32)
    for k in range(REL_BUCKETS):
        acc = jnp.where(bucket == k, tab_ref[:, k:k + 1], acc)
    o_ref[...] = acc


def _bias_vector(rel_table):
    return pl.pallas_call(
        _bvec_kernel,
        out_shape=jax.ShapeDtypeStruct((REL_HEADS, BVEC_LEN), F32),
        name="t5_bias_vector",
    )(rel_table.T)


GLA_TB = 256


def _log_sigmoid(z):
    return jnp.minimum(z, 0.0) - jnp.log(1.0 + jnp.exp(-jnp.abs(z)))


def _cumsum_rows(x):
    rows = x.shape[0]
    row = lax.broadcasted_iota(jnp.int32, x.shape, 0)
    sh = 1
    while sh < rows:
        x = x + jnp.where(row >= sh, pltpu.roll(x, sh, 0), 0.0)
        sh *= 2
    return x


def _gla_kernel(q_ref, k_ref, v_ref, r_ref, ga_ref, wg_ref, bg_ref, gn_ref, o_ref, s_ref, a_ref):
    c_id = pl.program_id(2)

    @pl.when(c_id == 0)
    def _():
        s_ref[...] = jnp.zeros_like(s_ref)

    C = GLA_CHUNK
    srow = lax.broadcasted_iota(jnp.int32, (2 * GLA_DK, 2 * GLA_DV), 0)
    scol = lax.broadcasted_iota(jnp.int32, (2 * GLA_DK, 2 * GLA_DV), 1)
    blockdiag = (srow < GLA_DK) == (scol < GLA_DV)
    trow = lax.broadcasted_iota(jnp.int32, (C, C), 0)
    tcol = lax.broadcasted_iota(jnp.int32, (C, C), 1)
    causal = tcol <= trow

    def chunk(c, carry):
        r0 = pl.multiple_of(c * C, C)
        q2 = q_ref[pl.ds(r0, C), :].astype(F32) * (GLA_DK ** -0.5)
        k2 = k_ref[pl.ds(r0, C), :].astype(F32)
        v2 = v_ref[pl.ds(r0, C), :]
        ga = ga_ref[pl.ds(r0, C), :][:, EX_GA:EX_GA + GLA_GATE_RANK]
        z = jnp.dot(ga.astype(BF16), wg_ref[...].astype(BF16), preferred_element_type=F32) + bg_ref[...]
        g = _log_sigmoid(z) / GLA_GATE_TAU
        G = _cumsum_rows(g)
        GT = G.T
        kT = k2.T
        qT = q2.T
        for t in range(C):
            dec = jnp.exp(GT[:, t:t + 1] - GT)
            prod = (qT[:, t:t + 1] * kT) * dec
            a_ref[0, t:t + 1, :] = jnp.sum(prod[:GLA_DK], axis=0, keepdims=True)
            a_ref[1, t:t + 1, :] = jnp.sum(prod[GLA_DK:], axis=0, keepdims=True)
        S = s_ref[...]
        o_inter = jnp.dot((q2 * jnp.exp(G)).astype(BF16), S.astype(BF16), preferred_element_type=F32)
        o_intra = []
        for h in range(2):
            A = jnp.where(causal, a_ref[h], 0.0)
            o_intra.append(jnp.dot(A.astype(BF16), v2[:, h * GLA_DV:(h + 1) * GLA_DV],
                                   preferred_element_type=F32))
        o2 = o_inter + jnp.concatenate(o_intra, axis=1)
        g_last = GT[:, C - 1:C]
        kd = kT * jnp.exp(g_last - GT)
        upd = jnp.dot(kd.astype(BF16), v2, preferred_element_type=F32)
        s_ref[...] = jnp.exp(g_last) * S + jnp.where(blockdiag, upd, 0.0)
        outs = []
        for h in range(2):
            of = o2[:, h * GLA_DV:(h + 1) * GLA_DV]
            of = of * lax.rsqrt(jnp.mean(of * of, axis=-1, keepdims=True) + LN_EPS) * gn_ref[...]
            rr = r_ref[pl.ds(r0, C), h * GLA_DV:(h + 1) * GLA_DV].astype(F32)
            outs.append(of * _silu(rr))
        o_ref[pl.ds(r0, C), :] = jnp.concatenate(outs, axis=1).astype(o_ref.dtype)
        return carry

    lax.fori_loop(0, GLA_TB // C, chunk, 0)


def _gla(main, extra, w_gate, b_gate, g_norm, batch, seq):
    nblk = seq // GLA_TB
    n = batch * seq

    def rows(b, hp, c):
        return b * nblk + c

    return pl.pallas_call(
        _gla_kernel,
        grid=(batch, 2, nblk),
        in_specs=[
            pl.BlockSpec((GLA_TB, 128), lambda b, hp, c: (rows(b, hp, c), AB_QA // 128 + hp)),
            pl.BlockSpec((GLA_TB, 128), lambda b, hp, c: (rows(b, hp, c), AB_KA // 128 + hp)),
            pl.BlockSpec((GLA_TB, 256), lambda b, hp, c: (rows(b, hp, c), AB_VA // 256 + hp)),
            pl.BlockSpec((GLA_TB, 256), lambda b, hp, c: (rows(b, hp, c), AB_RA // 256 + hp)),
            pl.BlockSpec((GLA_TB, 128), lambda b, hp, c: (rows(b, hp, c), 1)),
            pl.BlockSpec((GLA_GATE_RANK, 128), lambda b, hp, c: (0, hp)),
            pl.BlockSpec((1, 128), lambda b, hp, c: (0, hp)),
            pl.BlockSpec((1, GLA_DV), lambda b, hp, c: (0, 0)),
        ],
        out_specs=pl.BlockSpec((GLA_TB, 256), lambda b, hp, c: (rows(b, hp, c), hp)),
        out_shape=jax.ShapeDtypeStruct((n, GLA_HEADS * GLA_DV), BF16),
        scratch_shapes=[pltpu.VMEM((2 * GLA_DK, 2 * GLA_DV), F32),
                        pltpu.VMEM((2, GLA_CHUNK, GLA_CHUNK), F32)],
        compiler_params=_cparams(("parallel", "parallel", "arbitrary"), VMEM_LIMIT),
        name="gla",
    )(main, main, main, main, extra, w_gate, b_gate, g_norm)


ATT_TQ = 256
ATT_TK = 512


def _half_mask(shape, half):
    lane = lax.broadcasted_iota(jnp.int32, shape, len(shape) - 1)
    return (lane < HEAD_DIM) if half == 0 else (lane >= HEAD_DIM)


def _flash_step(s, v2, m, l, acc):
    m_new = jnp.maximum(m, jnp.max(s, axis=-1, keepdims=True))
    alpha = jnp.exp(m - m_new)
    p = jnp.exp(s - m_new)
    l = alpha * l + jnp.sum(p, axis=-1, keepdims=True)
    acc = alpha * acc + jnp.dot(p.astype(BF16), v2, preferred_element_type=F32)
    return m_new, l, acc


QK_SCALE = HEAD_DIM ** -0.5


def _qk(qm, k2):
    return lax.dot_general(qm, k2, (((1,), (1,)), ((), ())), preferred_element_type=F32)


def _load_kv(b, seq, srcs, dsts, sems):
    copies = []
    for n, (src, col, width) in enumerate(srcs):
        cp = pltpu.make_async_copy(src.at[pl.ds(b * seq, seq), pl.ds(col, width)], dsts[n], sems.at[n])
        cp.start()
        copies.append(cp)
    for cp in copies:
        cp.wait()


def _fcum_kernel(ex_ref, bf_ref, o_ref, carry_ref):
    i = pl.program_id(1)

    @pl.when(i == 0)
    def _():
        carry_ref[...] = jnp.zeros_like(carry_ref)

    lf = _log_sigmoid(ex_ref[...] + bf_ref[...])
    F = _cumsum_rows(lf) + carry_ref[...]
    o_ref[...] = F
    carry_ref[...] = F[F.shape[0] - 1:, :]


def _forget_cumsum(extra, b_forget_row, batch, seq, tb=256):
    nblk = seq // tb
    return pl.pallas_call(
        _fcum_kernel,
        grid=(batch, nblk),
        in_specs=[pl.BlockSpec((tb, 128), lambda b, i: (b * nblk + i, 0)),
                  pl.BlockSpec((1, 128), lambda b, i: (0, 0))],
        out_specs=pl.BlockSpec((tb, 128), lambda b, i: (b * nblk + i, 0)),
        out_shape=jax.ShapeDtypeStruct((batch * seq, 128), F32),
        scratch_shapes=[pltpu.VMEM((1, 128), F32)],
        compiler_params=_cparams(("parallel", "arbitrary")),
        name="forget_cumsum",
    )(extra, b_forget_row)


def _fox_kernel(q_ref, fq_ref, fk_ref, kv_hbm, o_ref, k_vm, v_vm, sems, *, seq):
    b = pl.program_id(0)
    i = pl.program_id(1)
    TQ, TK = ATT_TQ, ATT_TK

    @pl.when(i == 0)
    def _():
        _load_kv(b, seq, [(kv_hbm, CD_KC, 512), (kv_hbm, CD_VC, 512)], [k_vm, v_vm], sems)

    nj = (i * TQ + TQ - 1) // TK + 1
    row = i * TQ + lax.broadcasted_iota(jnp.int32, (TQ, TK), 0)
    col0 = lax.broadcasted_iota(jnp.int32, (TQ, TK), 1)
    for p in range(FOX_HEADS // 2):
        q2 = q_ref[:, p * 128:(p + 1) * 128]
        qms = [jnp.where(_half_mask((TQ, 128), half), q2 * QK_SCALE, jnp.zeros_like(q2)) for half in range(2)]
        fqs = [fq_ref[:, 2 * p + half:2 * p + half + 1] for half in range(2)]

        def body(j, carry, diagonal=False, qms=qms, fqs=fqs, p=p):
            k0 = pl.multiple_of(j * TK, TK)
            k2 = k_vm[pl.ds(k0, TK), p * 128:(p + 1) * 128]
            v2 = v_vm[pl.ds(k0, TK), p * 128:(p + 1) * 128]
            out = []
            for half in range(2):
                m, l, acc = carry[half]
                s = _qk(qms[half], k2) + (fqs[half] - fk_ref[2 * p + half, pl.ds(j, 1), :])
                if diagonal:
                    s = jnp.where(col0 + j * TK <= row, s, MASKED)
                out.append(_flash_step(s, v2, m, l, acc))
            return tuple(out)

        init = (jnp.full((TQ, 1), M_INIT, F32), jnp.zeros((TQ, 1), F32), jnp.zeros((TQ, 128), F32))
        carry = lax.fori_loop(0, nj - 1, body, (init, init))
        (_, l0, acc0), (_, l1, acc1) = body(nj - 1, carry, diagonal=True)
        o_ref[:, p * 128:(p + 1) * 128] = jnp.where(_half_mask((TQ, 128), 0), acc0 / l0, acc1 / l1).astype(o_ref.dtype)


def _fox(main, fcum, batch, seq):
    TQ, TK = ATT_TQ, ATT_TK
    nq = seq // TQ
    nk = seq // TK
    n = batch * seq
    fk = fcum[:, :FOX_HEADS].reshape(batch, seq, FOX_HEADS).transpose(0, 2, 1).reshape(batch * FOX_HEADS, nk, TK)
    return pl.pallas_call(
        functools.partial(_fox_kernel, seq=seq),
        grid=(batch, nq),
        in_specs=[pl.BlockSpec((TQ, 512), lambda b, i: (b * nq + i, CD_QC // 512)),
                  pl.BlockSpec((TQ, 128), lambda b, i: (b * nq + i, 0)),
                  pl.BlockSpec((FOX_HEADS, nk, TK), lambda b, i: (b, 0, 0)),
                  pl.BlockSpec(memory_space=pl.ANY)],
        out_specs=pl.BlockSpec((TQ, 512), lambda b, i: (b * nq + i, 0)),
        out_shape=jax.ShapeDtypeStruct((n, 512), BF16),
        scratch_shapes=[pltpu.VMEM((seq, 512), BF16), pltpu.VMEM((seq, 512), BF16),
                        pltpu.SemaphoreType.DMA((2,))],
        compiler_params=_cparams(("arbitrary", "arbitrary"), VMEM_LIMIT),
        name="fox_attention",
    )(main, fcum, fk, main)


INT_MIN = -(2 ** 31)
N_DELTA = 18


def _sortable(x):
    b = pltpu.bitcast(x, jnp.int32)
    return b ^ ((b >> 31) & 0x7FFFFFFF)


def _fold_lanes(x):
    out = x[:, 0:LANES]
    for c in range(1, x.shape[1] // LANES):
        out = out + x[:, c * LANES:(c + 1) * LANES]
    return out


def _dsa_kernel(q_ref, qi_ref, ex_ref, kv_hbm, kid_hbm, tz_hbm, o_ref,
                k_vm, v_vm, kid_vm, tz_vm, key_vm, qm_vm, qim_vm, mask_vm, m_vm, l_vm, acc_vm, cut_vm, sems,
                *, seq, topk):
    b = pl.program_id(0)
    i = pl.program_id(1)
    TQ, TK = ATT_TQ, ATT_TK

    @pl.when(i == 0)
    def _():
        _load_kv(b, seq, [(kv_hbm, AB_KB, 512), (kv_hbm, AB_VB, 512), (kid_hbm, 0, 128)],
                 [k_vm, v_vm, kid_vm], sems)

    @pl.when((i == 0) & (b == 0))
    def _():
        cp = pltpu.make_async_copy(tz_hbm, tz_vm, sems.at[3])
        cp.start()
        cp.wait()

    nj = (i * TQ + TQ - 1) // TK + 1
    row = i * TQ + lax.broadcasted_iota(jnp.int32, (TQ, TK), 0)
    col0 = lax.broadcasted_iota(jnp.int32, (TQ, TK), 1)

    for p in range(4):
        q2 = q_ref[:, p * 128:(p + 1) * 128]
        qi2 = qi_ref[:, p * 128:(p + 1) * 128]
        for half in range(2):
            keep = _half_mask((TQ, 128), half)
            qm_vm[2 * p + half] = jnp.where(keep, q2 * QK_SCALE, jnp.zeros_like(q2))
            qim_vm[2 * p + half] = jnp.where(keep, qi2, jnp.zeros_like(qi2))

    def score_chunk(j, carry):
        k0 = pl.multiple_of(j * TK, TK)
        kd = kid_vm[pl.ds(k0, TK), :].astype(BF16)
        score = jnp.zeros((TQ, TK), F32)
        for h in range(IDX_HEADS):
            s = lax.dot_general(qim_vm[h], kd, (((1,), (1,)), ((), ())), preferred_element_type=F32)
            w = ex_ref[:, EX_WI + h:EX_WI + h + 1]
            score = score + w * jnp.maximum(s, 0.0)
        score = jnp.where(col0 + j * TK <= row, score, -jnp.inf)
        key_vm[j] = _sortable(score)
        return carry

    lax.fori_loop(0, nj, score_chunk, 0)

    RH = 128
    assert seq // LANES <= 256
    ones_b = jnp.ones((LANES, LANES), BF16)

    def count_rows(hit_fn):
        parts = []
        for r0 in range(0, TQ, RH):
            def body(j, acc, r0=r0):
                for cb in range(TK // LANES):
                    blk = key_vm[j, r0:r0 + RH, cb * LANES:(cb + 1) * LANES]
                    acc = acc + jnp.where(hit_fn(blk, j, r0, cb), 1, 0)
                return acc
            parts.append(lax.fori_loop(0, nj, body, jnp.zeros((RH, LANES), jnp.int32)))
        acc = jnp.concatenate(parts, axis=0)
        tot = jnp.dot(acc.astype(F32).astype(BF16), ones_b, preferred_element_type=F32)
        return tot.astype(jnp.int32)

    def count_ge(c):
        return count_rows(lambda blk, j, r0, cb: blk >= c[r0:r0 + RH])

    ans = jnp.where(count_ge(jnp.zeros((TQ, LANES), jnp.int32)) >= topk, 0, INT_MIN).astype(jnp.int32)

    def bit_step(it, ans):
        cand = ans | (jnp.int32(1) << (30 - it))
        return jnp.where(count_ge(cand) >= topk, cand, ans)

    thr_d = lax.fori_loop(0, 31, bit_step, ans)
    thr = thr_d[:, 0:1]

    cut_vm[...] = jnp.full((TQ, 1), seq, jnp.int32)
    n_ge = count_ge(thr_d)

    @pl.when(jnp.max(n_ge) > topk)
    def _():
        need = topk - count_ge(thr_d + 1)
        colh = lax.broadcasted_iota(jnp.int32, (RH, LANES), 1)

        def count_eq_below(x):
            return count_rows(lambda blk, j, r0, cb: (blk == thr_d[r0:r0 + RH])
                              & (colh + (j * TK + cb * LANES) < x[r0:r0 + RH]))

        nbits = max(1, (seq - 1).bit_length())

        def cut_step(it, x):
            cand = x | (jnp.int32(1) << (nbits - 1 - it))
            return jnp.where(count_eq_below(cand) < need, cand, x)

        cut_d = lax.fori_loop(0, nbits, cut_step, jnp.zeros((TQ, LANES), jnp.int32))
        cut_vm[...] = cut_d[:, 0:1]

    cut = cut_vm[...]

    m_vm[...] = jnp.full(m_vm.shape, M_INIT, F32)
    l_vm[...] = jnp.zeros(l_vm.shape, F32)
    acc_vm[...] = jnp.zeros(acc_vm.shape, F32)

    def attend(j, carry):
        k0 = pl.multiple_of(j * TK, TK)
        col = col0 + j * TK
        key = key_vm[j]
        sel = ((key > thr) | ((key == thr) & (col <= cut))) & (col <= row)
        mask_vm[...] = jnp.where(sel, 0.0, MASKED)
        for h in range(DSA_HEADS):
            p = h // 2
            k2 = k_vm[pl.ds(k0, TK), p * 128:(p + 1) * 128]
            v2 = v_vm[pl.ds(k0, TK), p * 128:(p + 1) * 128]
            tiles = []
            for ra in range(TQ // 128):
                rowt = []
                for cb in range(TK // 128):
                    delta = jnp.clip(i * (TQ // 128) + ra - j * (TK // 128) - cb, 0, N_DELTA - 1)
                    rowt.append(tz_vm[delta, h])
                tiles.append(jnp.concatenate(rowt, axis=1))
            bias = jnp.concatenate(tiles, axis=0)
            s = _qk(qm_vm[h], k2) + bias + mask_vm[...]
            m, l, acc = _flash_step(s, v2, m_vm[h], l_vm[h], acc_vm[h])
            m_vm[h] = m
            l_vm[h] = l
            acc_vm[h] = acc
        return carry

    lax.fori_loop(0, nj, attend, 0)

    for p in range(4):
        o0 = acc_vm[2 * p] / l_vm[2 * p]
        o1 = acc_vm[2 * p + 1] / l_vm[2 * p + 1]
        o_ref[:, p * 128:(p + 1) * 128] = jnp.where(_half_mask((TQ, 128), 0), o0, o1).astype(o_ref.dtype)


def _dsa(main, extra, toeplitz, batch, seq):
    TQ, TK = ATT_TQ, ATT_TK
    nq = seq // TQ
    nk = seq // TK
    n = batch * seq
    topk = min(DSA_TOPK_MAX, seq // 4)
    return pl.pallas_call(
        functools.partial(_dsa_kernel, seq=seq, topk=topk),
        grid=(batch, nq),
        in_specs=[pl.BlockSpec((TQ, 512), lambda b, i: (b * nq + i, AB_QB // 512)),
                  pl.BlockSpec((TQ, 512), lambda b, i: (b * nq + i, AB_QI // 512)),
                  pl.BlockSpec((TQ, 128), lambda b, i: (b * nq + i, 1)),
                  pl.BlockSpec(memory_space=pl.ANY),
                  pl.BlockSpec(memory_space=pl.ANY),
                  pl.BlockSpec(memory_space=pl.ANY)],
        out_specs=pl.BlockSpec((TQ, 512), lambda b, i: (b * nq + i, 0)),
        out_shape=jax.ShapeDtypeStruct((n, 512), BF16),
        scratch_shapes=[pltpu.VMEM((seq, 512), BF16), pltpu.VMEM((seq, 512), BF16),
                        pltpu.VMEM((seq, 128), F32),
                        pltpu.VMEM((N_DELTA, DSA_HEADS, 128, 128), F32),
                        pltpu.VMEM((nk, TQ, TK), jnp.int32),
                        pltpu.VMEM((DSA_HEADS, TQ, 128), BF16), pltpu.VMEM((IDX_HEADS, TQ, 128), BF16),
                        pltpu.VMEM((TQ, TK), F32),
                        pltpu.VMEM((DSA_HEADS, TQ, 1), F32), pltpu.VMEM((DSA_HEADS, TQ, 1), F32),
                        pltpu.VMEM((DSA_HEADS, TQ, 128), F32),
                        pltpu.VMEM((TQ, 1), jnp.int32),
                        pltpu.SemaphoreType.DMA((4,))],
        compiler_params=_cparams(("arbitrary", "arbitrary"), VMEM_LIMIT),
        name="dsa_attention",
    )(main, main, extra, main, extra, toeplitz)


DIL_TQ = 128


def _dil_kernel(q_ref, kp_ref, kc_ref, vp_ref, vc_ref, bm_ref, o_ref, ld_ref):
    i = pl.program_id(2)
    TQ = DIL_TQ
    col = lax.broadcasted_iota(jnp.int32, (TQ, 2 * TQ), 1)
    first = jnp.where((i == 0) & (col < TQ), MASKED, 0.0)
    for p in range(DIL_HEADS // 2):
        sl = slice(p * 128, (p + 1) * 128)
        q2 = q_ref[:, sl]
        kc = jnp.concatenate([kp_ref[:, sl], kc_ref[:, sl]], axis=0)
        vc = jnp.concatenate([vp_ref[:, sl], vc_ref[:, sl]], axis=0)
        outs, lds = [], []
        for half in range(2):
            h = 2 * p + half
            qm = jnp.where(_half_mask((TQ, 128), half), q2 * QK_SCALE, jnp.zeros_like(q2))
            lg = _qk(qm, kc) + bm_ref[h] + first
            m = jnp.max(lg, axis=-1, keepdims=True)
            e = jnp.exp(lg - m)
            s = jnp.sum(e, axis=-1, keepdims=True)
            outs.append(jnp.dot(e.astype(BF16), vc, preferred_element_type=F32) / s)
            lds.append(jnp.broadcast_to(m + jnp.log(s), (TQ, 128)))
        lo = _half_mask((TQ, 128), 0)
        o_ref[:, sl] = jnp.where(lo, outs[0], outs[1])
        ld_ref[:, sl] = jnp.where(lo, lds[0], lds[1])


def _dilated_pattern(main, biasmask, dil, batch, seq):
    TQ = DIL_TQ
    n = batch * seq
    ld_seq = seq // dil
    nblk = ld_seq // TQ
    view = main.reshape(n // dil, dil * CD_MAIN)
    cpb = CD_MAIN // 512

    def qmap(b, r, i):
        return (b * nblk + i, r * cpb + CD_QD // 512)

    def kmap(off):
        def f(b, r, i):
            return (b * nblk + jnp.maximum(i - 1 + off, 0), r * cpb + CD_KD // 512)
        return f

    def vmap_(off):
        def f(b, r, i):
            return (b * nblk + jnp.maximum(i - 1 + off, 0), r * cpb + CD_VD // 512)
        return f

    o, ld = pl.pallas_call(
        _dil_kernel,
        grid=(batch, dil, nblk),
        in_specs=[pl.BlockSpec((TQ, 512), qmap),
                  pl.BlockSpec((TQ, 512), kmap(0)), pl.BlockSpec((TQ, 512), kmap(1)),
                  pl.BlockSpec((TQ, 512), vmap_(0)), pl.BlockSpec((TQ, 512), vmap_(1)),
                  pl.BlockSpec((DIL_HEADS, TQ, 2 * TQ), lambda b, r, i: (0, 0, 0))],
        out_specs=[pl.BlockSpec((TQ, 512), lambda b, r, i: (b * nblk + i, r)),
                   pl.BlockSpec((TQ, 512), lambda b, r, i: (b * nblk + i, r))],
        out_shape=[jax.ShapeDtypeStruct((n // dil, dil * 512), F32),
                   jax.ShapeDtypeStruct((n // dil, dil * 512), F32)],
        compiler_params=_cparams(("parallel", "parallel", "parallel"), VMEM_LIMIT),
        name="dilated_attention",
    )(view, view, view, view, view, biasmask)
    return o.reshape(n, 512), ld.reshape(n, 512)


def _dil_merge_kernel(o0, l0, o1, l1, o2, l2, out_ref):
    a, b, c = l0[...], l1[...], l2[...]
    mx = jnp.maximum(jnp.maximum(a, b), c)
    ea, eb, ec = jnp.exp(a - mx), jnp.exp(b - mx), jnp.exp(c - mx)
    tot = ea + eb + ec
    out_ref[...] = ((ea * o0[...] + eb * o1[...] + ec * o2[...]) / tot).astype(out_ref.dtype)


def _dilated(main, bvec, batch, seq):
    n = batch * seq
    parts = []
    a = jnp.arange(DIL_TQ)[:, None]
    c = jnp.arange(2 * DIL_TQ)[None, :]
    off = a - c + DIL_TQ
    inside = (off >= 0) & (off <= DIL_TQ)
    for window, dil in DIL_PATTERNS:
        assert window // dil == DIL_TQ
        bias = bvec[:, BVEC_SHIFT + dil * jnp.clip(off, 0, DIL_TQ)]
        biasmask = jnp.where(inside[None], bias, MASKED)
        parts.extend(_dilated_pattern(main, biasmask, dil, batch, seq))
    tm = 512
    spec = pl.BlockSpec((tm, 512), lambda i: (i, 0))
    return pl.pallas_call(
        _dil_merge_kernel,
        grid=(n // tm,),
        in_specs=[spec] * 6,
        out_specs=spec,
        out_shape=jax.ShapeDtypeStruct((n, 512), BF16),
        compiler_params=_cparams(("parallel",)),
        name="dilated_merge",
    )(*parts)


MOE_TM = 512
MOE_TF = 512


def _split3(a):
    hi = a.astype(BF16)
    r1 = a - hi.astype(F32)
    mid = r1.astype(BF16)
    lo = (r1 - mid.astype(F32)).astype(BF16)
    return hi, mid, lo


def _router_kernel(x_ref, w_ref, idx_ref, gate_ref):
    xh, xm, xl = _split3(x_ref[...])
    wh, wm, wl = _split3(w_ref[...])
    dot = functools.partial(jnp.dot, preferred_element_type=F32)
    logits = (dot(xh, wh) + (dot(xh, wm) + dot(xm, wh))
              + (dot(xh, wl) + dot(xm, wm) + dot(xl, wh)))
    lane = lax.broadcasted_iota(jnp.int32, logits.shape, 1)
    lg = jnp.where(lane < N_EXPERTS, logits, -jnp.inf)
    v1 = jnp.max(lg, axis=-1, keepdims=True)
    i1 = jnp.min(jnp.where(lg == v1, lane, LANES), axis=-1, keepdims=True)
    lg2 = jnp.where(lane == i1, -jnp.inf, lg)
    v2 = jnp.max(lg2, axis=-1, keepdims=True)
    i2 = jnp.min(jnp.where(lg2 == v2, lane, LANES), axis=-1, keepdims=True)
    e2 = jnp.exp(v2 - v1)
    den = 1.0 + e2
    idx_ref[...] = jnp.where(lane == 0, i1, jnp.where(lane == 1, i2, 0))
    gate_ref[...] = jnp.where(lane == 0, 1.0 / den, jnp.where(lane == 1, e2 / den, 0.0))


def _router(x, w_router_pad, tm=512):
    n, d = x.shape
    return pl.pallas_call(
        _router_kernel,
        grid=(n // tm,),
        in_specs=[pl.BlockSpec((tm, d), lambda i: (i, 0)),
                  pl.BlockSpec((d, LANES), lambda i: (0, 0))],
        out_specs=[pl.BlockSpec((tm, LANES), lambda i: (i, 0)),
                   pl.BlockSpec((tm, LANES), lambda i: (i, 0))],
        out_shape=[jax.ShapeDtypeStruct((n, LANES), jnp.int32),
                   jax.ShapeDtypeStruct((n, LANES), F32)],
        compiler_params=_cparams(("parallel",), VMEM_LIMIT),
        name="moe_router",
    )(x, w_router_pad)


MOE_TC = 512
MOE_WIN = 640


def _moe_ffn_kernel(te_ref, nv_ref, clo_ref, chi_ref, src_ref, x_hbm, w1_ref, w3_ref, w2_ref, gate_ref,
                    hi_ref, lo_ref, acc_ref, xs_ref, xbuf, sems):
    t = pl.program_id(0)
    f = pl.program_id(1)
    last = pl.num_programs(1) - 1
    live = t < nv_ref[0]
    tm = acc_ref.shape[0]

    def fetch(c, slot):
        return pltpu.make_async_copy(x_hbm.at[pl.ds(c * MOE_TC, MOE_TC)], xbuf.at[slot], sems.at[slot])

    @pl.when(live & (f == 0))
    def _():
        clo = clo_ref[t]
        chi = chi_ref[t]
        acc_ref[...] = jnp.zeros_like(acc_ref)
        fetch(clo, 0).start()
        src = src_ref[...]
        lane = lax.broadcasted_iota(jnp.int32, (tm, MOE_TC), 1)

        def body(c, carry):
            slot = (c - clo) & 1

            @pl.when(c < chi)
            def _():
                fetch(c + 1, 1 - slot).start()

            fetch(c, slot).wait()
            onehot = jnp.where(src == lane + c * MOE_TC, 1.0, 0.0).astype(BF16)
            acc_ref[...] += jnp.dot(onehot, xbuf[slot], preferred_element_type=F32)
            return carry

        lax.fori_loop(clo, chi + 1, body, 0)
        xs_ref[...] = acc_ref[...].astype(BF16)

    @pl.when(f == 0)
    def _():
        acc_ref[...] = jnp.zeros_like(acc_ref)

    @pl.when(live)
    def _():
        xb = xs_ref[...]
        a = jnp.dot(xb, w1_ref[...], preferred_element_type=F32)
        c = jnp.dot(xb, w3_ref[...], preferred_element_type=F32)
        h = (_silu(a) * c).astype(BF16)
        acc_ref[...] += jnp.dot(h, w2_ref[...], preferred_element_type=F32)

    @pl.when(f == last)
    def _():
        y = acc_ref[...] * gate_ref[...]
        hi = y.astype(BF16)
        hi_ref[...] = hi
        lo_ref[...] = (y - hi.astype(F32)).astype(BF16)


def _moe_ffn(xb, row_src, w1, w3, w2, gate_sorted, tile_expert, n_live, chunk_lo, chunk_hi):
    p_rows = row_src.shape[0]
    d = xb.shape[1]
    ff = w1.shape[2]
    tm, tf = MOE_TM, MOE_TF

    def rows(t, f, te, nv, clo, chi):
        return (t, 0)

    return pl.pallas_call(
        _moe_ffn_kernel,
        grid_spec=pltpu.PrefetchScalarGridSpec(
            num_scalar_prefetch=4,
            grid=(p_rows // tm, ff // tf),
            in_specs=[pl.BlockSpec((tm, 1), rows),
                      pl.BlockSpec(memory_space=pl.ANY),
                      pl.BlockSpec((None, d, tf), lambda t, f, te, nv, clo, chi: (te[t], 0, f)),
                      pl.BlockSpec((None, d, tf), lambda t, f, te, nv, clo, chi: (te[t], 0, f)),
                      pl.BlockSpec((None, tf, d), lambda t, f, te, nv, clo, chi: (te[t], f, 0)),
                      pl.BlockSpec((tm, 1), rows)],
            out_specs=[pl.BlockSpec((tm, d), rows), pl.BlockSpec((tm, d), rows)],
            scratch_shapes=[pltpu.VMEM((tm, d), F32), pltpu.VMEM((tm, d), BF16),
                            pltpu.VMEM((2, MOE_TC, d), BF16), pltpu.SemaphoreType.DMA((2,))],
        ),
        out_shape=[jax.ShapeDtypeStruct((p_rows, d), BF16), jax.ShapeDtypeStruct((p_rows, d), BF16)],
        compiler_params=_cparams(("arbitrary", "arbitrary"), VMEM_LIMIT),
        name="moe_grouped_swiglu",
    )(tile_expert, n_live, chunk_lo, chunk_hi, row_src, xb, w1, w3, w2, gate_sorted)


def _combine_ln_kernel(win_ref, pos_ref, eid_ref, hi_hbm, lo_hbm, x_ref, g_ref, b_ref, xf_ref, xb_ref,
                       acc_ref, hbuf, lbuf, sems):
    blk = pl.program_id(0)
    tm = acc_ref.shape[0]

    def fetch(e, slot):
        w = pl.multiple_of(win_ref[blk * N_EXPERTS + e], 16)
        return (pltpu.make_async_copy(hi_hbm.at[pl.ds(w, MOE_WIN)], hbuf.at[slot], sems.at[0, slot]),
                pltpu.make_async_copy(lo_hbm.at[pl.ds(w, MOE_WIN)], lbuf.at[slot], sems.at[1, slot]))

    for cp in fetch(0, 0):
        cp.start()
    pos0 = pos_ref[:, 0:1]
    pos1 = pos_ref[:, 1:2]
    lane = lax.broadcasted_iota(jnp.int32, (tm, MOE_WIN), 1)
    acc_ref[...] = ALPHA * x_ref[...]
    for e in range(N_EXPERTS):
        slot = e & 1
        if e + 1 < N_EXPERTS:
            for cp in fetch(e + 1, 1 - slot):
                cp.start()
        for cp in fetch(e, slot):
            cp.wait()
        w = win_ref[blk * N_EXPERTS + e]
        rel0 = jnp.where(eid_ref[:, 0:1] == e, pos0 - w, -1)
        rel1 = jnp.where(eid_ref[:, 1:2] == e, pos1 - w, -1)
        onehot = jnp.where((lane == rel0) | (lane == rel1), 1.0, 0.0).astype(BF16)
        acc_ref[...] += (jnp.dot(onehot, hbuf[slot], preferred_element_type=F32)
                         + jnp.dot(onehot, lbuf[slot], preferred_element_type=F32))
    y = _layernorm_rows(acc_ref[...], g_ref[...], b_ref[...])
    xf_ref[...] = y
    xb_ref[...] = y.astype(BF16)


def _combine_ln(win, pos2, eid2, ys_hi, ys_lo, x, g, b):
    n, d = x.shape
    tm = MOE_TC
    return pl.pallas_call(
        _combine_ln_kernel,
        grid_spec=pltpu.PrefetchScalarGridSpec(
            num_scalar_prefetch=1,
            grid=(n // tm,),
            in_specs=[pl.BlockSpec((tm, TOP_K), lambda i, w: (i, 0)),
                      pl.BlockSpec((tm, TOP_K), lambda i, w: (i, 0)),
                      pl.BlockSpec(memory_space=pl.ANY),
                      pl.BlockSpec(memory_space=pl.ANY),
                      pl.BlockSpec((tm, d), lambda i, w: (i, 0)),
                      pl.BlockSpec((1, d), lambda i, w: (0, 0)),
                      pl.BlockSpec((1, d), lambda i, w: (0, 0))],
            out_specs=[pl.BlockSpec((tm, d), lambda i, w: (i, 0)),
                       pl.BlockSpec((tm, d), lambda i, w: (i, 0))],
            scratch_shapes=[pltpu.VMEM((tm, d), F32),
                            pltpu.VMEM((2, MOE_WIN, d), BF16), pltpu.VMEM((2, MOE_WIN, d), BF16),
                            pltpu.SemaphoreType.DMA((2, 2))],
        ),
        out_shape=[jax.ShapeDtypeStruct((n, d), F32), jax.ShapeDtypeStruct((n, d), BF16)],
        compiler_params=_cparams(("arbitrary",), VMEM_LIMIT),
        name="moe_combine_ln",
    )(win, pos2, eid2, ys_hi, ys_lo, x, g, b)


def _moe(x, xb, w_router_pad, w1, w3, w2, g, b):
    n, d = x.shape
    tm = MOE_TM
    idx, gates = _router(x, w_router_pad)
    e_flat = idx[:, :TOP_K].reshape(-1)
    g_flat = gates[:, :TOP_K].reshape(-1)
    onehot = (e_flat[:, None] == jnp.arange(N_EXPERTS, dtype=jnp.int32)[None, :]).astype(jnp.int32)
    csum = jnp.cumsum(onehot, axis=0)
    rank = jnp.take_along_axis(csum, e_flat[:, None], axis=1)[:, 0] - 1
    counts = csum[-1]
    padded = ((counts + tm - 1) // tm) * tm
    ends = jnp.cumsum(padded)
    pos = ((ends - padded)[e_flat] + rank).astype(jnp.int32)
    tiles = (TOP_K * n) // tm + N_EXPERTS + 2
    p_rows = tiles * tm
    token = jnp.arange(TOP_K * n, dtype=jnp.int32) // TOP_K
    row_src = jnp.full((p_rows,), -1, jnp.int32).at[pos].set(token)
    gate_sorted = jnp.zeros((p_rows,), F32).at[pos].set(g_flat).reshape(p_rows, 1)
    tile_start = jnp.arange(tiles, dtype=jnp.int32) * tm
    tile_expert = jnp.minimum(jnp.searchsorted(ends, tile_start, side="right"), N_EXPERTS - 1).astype(jnp.int32)
    n_live = (ends[-1] // tm).astype(jnp.int32).reshape(1)
    per_tile = row_src.reshape(tiles, tm)
    chunk_lo = (jnp.min(jnp.where(per_tile >= 0, per_tile, n - 1), axis=1) // MOE_TC).astype(jnp.int32)
    chunk_hi = (jnp.max(per_tile, axis=1) // MOE_TC).astype(jnp.int32)
    big = jnp.int32(p_rows)
    wkey = (token // MOE_TC) * N_EXPERTS + e_flat
    win = jnp.full(((n // MOE_TC) * N_EXPERTS,), big, jnp.int32).at[wkey].min(pos)
    win = (jnp.where(win == big, 0, win) // 16) * 16

    ys_hi, ys_lo = _moe_ffn(xb, row_src.reshape(p_rows, 1), w1, w3, w2, gate_sorted, tile_expert, n_live,
                            chunk_lo, chunk_hi)
    return _combine_ln(win, pos.reshape(n, TOP_K), idx[:, :TOP_K], ys_hi, ys_lo, x, g, b)


def _regroup_ab(w):
    main = jnp.concatenate([w[:, 0:1536], w[:, 1552:3600]], axis=1)
    ki = w[:, 3600:3664]
    extra = jnp.concatenate([ki, ki, w[:, 1536:1552], w[:, 3664:3672],
                             jnp.zeros((w.shape[0], 128 - GLA_GATE_RANK - IDX_HEADS), w.dtype)], axis=1)
    return main.astype(BF16), extra.astype(BF16)


def _regroup_cd(w):
    main = jnp.concatenate([w[:, 0:1536], w[:, 1544:3080]], axis=1)
    extra = jnp.concatenate([w[:, 1536:1544], jnp.zeros((w.shape[0], 128 - FOX_HEADS), w.dtype)], axis=1)
    return main.astype(BF16), extra.astype(BF16)


def _toeplitz_tiles(bvec):
    t = 128
    seg = jnp.stack([bvec[:, dl * t:dl * t + 2 * t - 1] for dl in range(N_DELTA - 1)], axis=1)
    w = jnp.flip(seg, axis=-1)
    w = jnp.concatenate([w, w[..., :1]], axis=-1)
    skew = jnp.tile(w, (1, 1, t))[..., :t * (2 * t - 1)].reshape(REL_HEADS, N_DELTA - 1, t, 2 * t - 1)
    near = skew[..., t - 1:]
    far = jnp.broadcast_to(bvec[:, BVEC_LEN - 1][:, None, None, None], (REL_HEADS, 1, t, t))
    return jnp.concatenate([near, far], axis=1).transpose(1, 0, 2, 3)


def kernel(x, ln_g, ln_b, rel_table, w_in_ab, w_gate_a, b_gate_a, g_norm_a, w_out_ab, w_in_cd, b_forget,
           w_out_cd, w1_dense, w3_dense, w2_dense, w_router, w1_moe, w3_moe, w2_moe):
    batch, seq, d = x.shape
    n = batch * seq
    xf = x.reshape(n, d)
    xb = xf.astype(BF16)

    bvec = _bias_vector(rel_table)
    toeplitz = _toeplitz_tiles(bvec)

    for layer in range(DEPTH):
        j = layer // 2
        g0, b0 = ln_g[layer, 0][None, :], ln_b[layer, 0][None, :]
        g1, b1 = ln_g[layer, 1][None, :], ln_b[layer, 1][None, :]
        if layer % 2 == 0:
            w_main, w_extra = _regroup_ab(w_in_ab[j])
            main = _matmul(xb, w_main, BF16, 512, 512)
            extra = _matmul(xb, w_extra, F32, 512, AB_EXTRA)
            oa = _gla(main, extra, w_gate_a[j], b_gate_a[j][None, :], g_norm_a[j][None, :], batch, seq)
            ob = _dsa(main, extra, toeplitz, batch, seq)
            o = jnp.concatenate([oa, ob], axis=1)
            xf, xb = _outproj_ln(o, w_out_ab[j].astype(BF16), xf, g0, b0)
            xf, xb = _swiglu_ln(xb, w1_dense[j].astype(BF16), w3_dense[j].astype(BF16),
                                w2_dense[j].astype(BF16), xf, g1, b1)
        else:
            w_main, w_extra = _regroup_cd(w_in_cd[j])
            main = _matmul(xb, w_main, BF16, 512, 512)
            extra = _matmul(xb, w_extra, F32, 512, CD_EXTRA)
            bf_row = jnp.concatenate([b_forget[j], jnp.zeros((128 - FOX_HEADS,), F32)])[None, :]
            fcum = _forget_cumsum(extra, bf_row, batch, seq)
            oc = _fox(main, fcum, batch, seq)
            od = _dilated(main, bvec, batch, seq)
            o = jnp.concatenate([oc, od], axis=1)
            xf, xb = _outproj_ln(o, w_out_cd[j].astype(BF16), xf, g0, b0)
            w_r = jnp.concatenate([w_router[j], jnp.zeros((d, LANES - N_EXPERTS), F32)], axis=1)
            xf, xb = _moe(xf, xb, w_r, w1_moe[j].astype(BF16), w3_moe[j].astype(BF16),
                          w2_moe[j].astype(BF16), g1, b1)
    return xf.reshape(batch, seq, d)
```

```python
import functools
import math

import jax
import jax.numpy as jnp
from jax import lax
from jax.experimental import pallas as pl
from jax.experimental.pallas import tpu as pltpu

D_MODEL = 1024
DEPTH = 4
HEAD_DIM = 64
GLA_HEADS = 4
GLA_DK = 64
GLA_DV = 128
GLA_GATE_RANK = 16
GLA_GATE_TAU = 16.0
GLA_CHUNK = 64
DSA_HEADS = 8
IDX_HEADS = 8
IDX_DIM = 64
DSA_TOPK_MAX = 256
FOX_HEADS = 8
DIL_HEADS = 8
DIL_PATTERNS = ((128, 1), (512, 4), (2048, 16))
REL_BUCKETS = 32
REL_MAX_DIST = 2048
REL_HEADS = 8
D_FF = 2816
N_EXPERTS = 8
TOP_K = 2
D_FF_EXPERT = 3584
ALPHA = (2 * DEPTH) ** 0.25
LN_EPS = 1e-5

LANES = 128
MASKED = -1e30
M_INIT = -1e20
VMEM_LIMIT = 56 * 1024 * 1024

F32 = jnp.float32
BF16 = jnp.bfloat16

AB_QA, AB_KA, AB_VA, AB_RA, AB_QB, AB_KB, AB_VB, AB_QI = 0, 256, 512, 1024, 1536, 2048, 2560, 3072
AB_MAIN = 3584
AB_EXTRA = 256
EX_GA = 0
EX_WI = 16
CD_QC, CD_KC, CD_VC, CD_QD, CD_KD, CD_VD = 0, 512, 1024, 1536, 2048, 2560
CD_MAIN = 3072
CD_EXTRA = 128


def _cparams(sem, vmem=None):
    return pltpu.CompilerParams(dimension_semantics=sem, vmem_limit_bytes=vmem)


def _mm_kernel(x_ref, w_ref, o_ref):
    o_ref[...] = jnp.dot(x_ref[...], w_ref[...], preferred_element_type=F32).astype(o_ref.dtype)


def _matmul(x, w, out_dtype, tm, tn):
    n, k = x.shape
    m = w.shape[1]
    return pl.pallas_call(
        _mm_kernel,
        grid=(n // tm, m // tn),
        in_specs=[pl.BlockSpec((tm, k), lambda i, j: (i, 0)),
                  pl.BlockSpec((k, tn), lambda i, j: (0, j))],
        out_specs=pl.BlockSpec((tm, tn), lambda i, j: (i, j)),
        out_shape=jax.ShapeDtypeStruct((n, m), out_dtype),
        compiler_params=_cparams(("parallel", "parallel"), VMEM_LIMIT),
        name="proj_matmul",
    )(x, w)


def _layernorm_rows(z, g, b):
    mu = jnp.mean(z, axis=-1, keepdims=True)
    zc = z - mu
    var = jnp.mean(zc * zc, axis=-1, keepdims=True)
    return zc * lax.rsqrt(var + LN_EPS) * g + b


def _outproj_ln_kernel(o_ref, w_ref, x_ref, g_ref, b_ref, xf_ref, xb_ref):
    mix = jnp.dot(o_ref[...], w_ref[...], preferred_element_type=F32)
    y = _layernorm_rows(ALPHA * x_ref[...] + mix, g_ref[...], b_ref[...])
    xf_ref[...] = y
    xb_ref[...] = y.astype(BF16)


def _outproj_ln(o, w, x, g, b, tm=256):
    n, k = o.shape
    d = w.shape[1]
    return pl.pallas_call(
        _outproj_ln_kernel,
        grid=(n // tm,),
        in_specs=[pl.BlockSpec((tm, k), lambda i: (i, 0)),
                  pl.BlockSpec((k, d), lambda i: (0, 0)),
                  pl.BlockSpec((tm, d), lambda i: (i, 0)),
                  pl.BlockSpec((1, d), lambda i: (0, 0)),
                  pl.BlockSpec((1, d), lambda i: (0, 0))],
        out_specs=[pl.BlockSpec((tm, d), lambda i: (i, 0)),
                   pl.BlockSpec((tm, d), lambda i: (i, 0))],
        out_shape=[jax.ShapeDtypeStruct((n, d), F32), jax.ShapeDtypeStruct((n, d), BF16)],
        compiler_params=_cparams(("parallel",), VMEM_LIMIT),
        name="outproj_ln",
    )(o, w, x, g, b)


def _silu(a):
    return a / (1.0 + jnp.exp(-a))


def _swiglu_ln_kernel(xb_ref, w1_ref, w3_ref, w2_ref, x_ref, g_ref, b_ref, xf_ref, xbo_ref, acc_ref):
    f = pl.program_id(1)

    @pl.when(f == 0)
    def _():
        acc_ref[...] = jnp.zeros_like(acc_ref)

    xb = xb_ref[...]
    a = jnp.dot(xb, w1_ref[...], preferred_element_type=F32)
    c = jnp.dot(xb, w3_ref[...], preferred_element_type=F32)
    h = (_silu(a) * c).astype(BF16)
    acc_ref[...] += jnp.dot(h, w2_ref[...], preferred_element_type=F32)

    @pl.when(f == pl.num_programs(1) - 1)
    def _():
        y = _layernorm_rows(ALPHA * x_ref[...] + acc_ref[...], g_ref[...], b_ref[...])
        xf_ref[...] = y
        xbo_ref[...] = y.astype(BF16)


def _swiglu_ln(xb, w1, w3, w2, x, g, b, tm=512, tf=1408):
    n, d = xb.shape
    ff = w1.shape[1]
    return pl.pallas_call(
        _swiglu_ln_kernel,
        grid=(n // tm, ff // tf),
        in_specs=[pl.BlockSpec((tm, d), lambda i, f: (i, 0)),
                  pl.BlockSpec((d, tf), lambda i, f: (0, f)),
                  pl.BlockSpec((d, tf), lambda i, f: (0, f)),
                  pl.BlockSpec((tf, d), lambda i, f: (f, 0)),
                  pl.BlockSpec((tm, d), lambda i, f: (i, 0)),
                  pl.BlockSpec((1, d), lambda i, f: (0, 0)),
                  pl.BlockSpec((1, d), lambda i, f: (0, 0))],
        out_specs=[pl.BlockSpec((tm, d), lambda i, f: (i, 0)),
                   pl.BlockSpec((tm, d), lambda i, f: (i, 0))],
        out_shape=[jax.ShapeDtypeStruct((n, d), F32), jax.ShapeDtypeStruct((n, d), BF16)],
        scratch_shapes=[pltpu.VMEM((tm, d), F32)],
        compiler_params=_cparams(("parallel", "arbitrary"), VMEM_LIMIT),
        name="swiglu_ln",
    )(xb, w1, w3, w2, x, g, b)


BVEC_LEN = 2304
BVEC_SHIFT = 127


def _bvec_kernel(tab_ref, o_ref):
    n = lax.broadcasted_iota(jnp.int32, (REL_HEADS, BVEC_LEN), 1)
    d = jnp.maximum(n - BVEC_SHIFT, 0)
    max_exact = REL_BUCKETS // 2
    df = jnp.maximum(d, 1).astype(F32)
    large = max_exact + (jnp.log(df / max_exact) / math.log(REL_MAX_DIST / max_exact)
                         * (REL_BUCKETS - max_exact)).astype(jnp.int32)
    large = jnp.minimum(large, REL_BUCKETS - 1)
    bucket = jnp.where(d < max_exact, d, large)
    acc = jnp.zeros((REL_HEADS, BVEC_LEN), F32)
    for k in range(REL_BUCKETS):
        acc = jnp.where(bucket == k, tab_ref[:, k:k + 1], acc)
    o_ref[...] = acc


def _bias_vector(rel_table):
    return pl.pallas_call(
        _bvec_kernel,
        out_shape=jax.ShapeDtypeStruct((REL_HEADS, BVEC_LEN), F32),
        name="t5_bias_vector",
    )(rel_table.T)


GLA_TB = 256


def _log_sigmoid(z):
    return jnp.minimum(z, 0.0) - jnp.log(1.0 + jnp.exp(-jnp.abs(z)))


def _cumsum_rows(x):
    rows = x.shape[0]
    row = lax.broadcasted_iota(jnp.int32, x.shape, 0)
    sh = 1
    while sh < rows:
        x = x + jnp.where(row >= sh, pltpu.roll(x, sh, 0), 0.0)
        sh *= 2
    return x


def _gla_kernel(q_ref, k_ref, v_ref, r_ref, ga_ref, wg_ref, bg_ref, gn_ref, o_ref, s_ref, a_ref):
    c_id = pl.program_id(2)

    @pl.when(c_id == 0)
    def _():
        s_ref[...] = jnp.zeros_like(s_ref)

    C = GLA_CHUNK
    srow = lax.broadcasted_iota(jnp.int32, (2 * GLA_DK, 2 * GLA_DV), 0)
    scol = lax.broadcasted_iota(jnp.int32, (2 * GLA_DK, 2 * GLA_DV), 1)
    blockdiag = (srow < GLA_DK) == (scol < GLA_DV)
    trow = lax.broadcasted_iota(jnp.int32, (C, C), 0)
    tcol = lax.broadcasted_iota(jnp.int32, (C, C), 1)
    causal = tcol <= trow

    def chunk(c, carry):
        r0 = pl.multiple_of(c * C, C)
        q2 = q_ref[pl.ds(r0, C), :].astype(F32) * (GLA_DK ** -0.5)
        k2 = k_ref[pl.ds(r0, C), :].astype(F32)
        v2 = v_ref[pl.ds(r0, C), :]
        ga = ga_ref[pl.ds(r0, C), :][:, EX_GA:EX_GA + GLA_GATE_RANK]
        z = jnp.dot(ga.astype(BF16), wg_ref[...].astype(BF16), preferred_element_type=F32) + bg_ref[...]
        g = _log_sigmoid(z) / GLA_GATE_TAU
        G = _cumsum_rows(g)
        GT = G.T
        kT = k2.T
        qT = q2.T
        for t in range(C):
            dec = jnp.exp(GT[:, t:t + 1] - GT)
            prod = (qT[:, t:t + 1] * kT) * dec
            a_ref[0, t:t + 1, :] = jnp.sum(prod[:GLA_DK], axis=0, keepdims=True)
            a_ref[1, t:t + 1, :] = jnp.sum(prod[GLA_DK:], axis=0, keepdims=True)
        S = s_ref[...]
        o_inter = jnp.dot((q2 * jnp.exp(G)).astype(BF16), S.astype(BF16), preferred_element_type=F32)
        o_intra = []
        for h in range(2):
            A = jnp.where(causal, a_ref[h], 0.0)
            o_intra.append(jnp.dot(A.astype(BF16), v2[:, h * GLA_DV:(h + 1) * GLA_DV],
                                   preferred_element_type=F32))
        o2 = o_inter + jnp.concatenate(o_intra, axis=1)
        g_last = GT[:, C - 1:C]
        kd = kT * jnp.exp(g_last - GT)
        upd = jnp.dot(kd.astype(BF16), v2, preferred_element_type=F32)
        s_ref[...] = jnp.exp(g_last) * S + jnp.where(blockdiag, upd, 0.0)
        outs = []
        for h in range(2):
            of = o2[:, h * GLA_DV:(h + 1) * GLA_DV]
            of = of * lax.rsqrt(jnp.mean(of * of, axis=-1, keepdims=True) + LN_EPS) * gn_ref[...]
            rr = r_ref[pl.ds(r0, C), h * GLA_DV:(h + 1) * GLA_DV].astype(F32)
            outs.append(of * _silu(rr))
        o_ref[pl.ds(r0, C), :] = jnp.concatenate(outs, axis=1).astype(o_ref.dtype)
        return carry

    lax.fori_loop(0, GLA_TB // C, chunk, 0)


def _gla(main, extra, w_gate, b_gate, g_norm, batch, seq):
    nblk = seq // GLA_TB
    n = batch * seq

    def rows(b, hp, c):
        return b * nblk + c

    return pl.pallas_call(
        _gla_kernel,
        grid=(batch, 2, nblk),
        in_specs=[
            pl.BlockSpec((GLA_TB, 128), lambda b, hp, c: (rows(b, hp, c), AB_QA // 128 + hp)),
            pl.BlockSpec((GLA_TB, 128), lambda b, hp, c: (rows(b, hp, c), AB_KA // 128 + hp)),
            pl.BlockSpec((GLA_TB, 256), lambda b, hp, c: (rows(b, hp, c), AB_VA // 256 + hp)),
            pl.BlockSpec((GLA_TB, 256), lambda b, hp, c: (rows(b, hp, c), AB_RA // 256 + hp)),
            pl.BlockSpec((GLA_TB, 128), lambda b, hp, c: (rows(b, hp, c), 1)),
            pl.BlockSpec((GLA_GATE_RANK, 128), lambda b, hp, c: (0, hp)),
            pl.BlockSpec((1, 128), lambda b, hp, c: (0, hp)),
            pl.BlockSpec((1, GLA_DV), lambda b, hp, c: (0, 0)),
        ],
        out_specs=pl.BlockSpec((GLA_TB, 256), lambda b, hp, c: (rows(b, hp, c), hp)),
        out_shape=jax.ShapeDtypeStruct((n, GLA_HEADS * GLA_DV), BF16),
        scratch_shapes=[pltpu.VMEM((2 * GLA_DK, 2 * GLA_DV), F32),
                        pltpu.VMEM((2, GLA_CHUNK, GLA_CHUNK), F32)],
        compiler_params=_cparams(("parallel", "parallel", "arbitrary"), VMEM_LIMIT),
        name="gla",
    )(main, main, main, main, extra, w_gate, b_gate, g_norm)


ATT_TQ = 256
ATT_TK = 1024
FOX_TQ = 512


def _half_mask(shape, half):
    lane = lax.broadcasted_iota(jnp.int32, shape, len(shape) - 1)
    return (lane < HEAD_DIM) if half == 0 else (lane >= HEAD_DIM)


def _flash_step(s, v2, m, l, acc):
    m_new = jnp.maximum(m, jnp.max(s, axis=-1, keepdims=True))
    alpha = jnp.exp(m - m_new)
    p = jnp.exp(s - m_new)
    l = alpha * l + jnp.sum(p, axis=-1, keepdims=True)
    acc = alpha * acc + jnp.dot(p.astype(BF16), v2, preferred_element_type=F32)
    return m_new, l, acc


QK_SCALE = HEAD_DIM ** -0.5


def _qk(qm, k2):
    return lax.dot_general(qm, k2, (((1,), (1,)), ((), ())), preferred_element_type=F32)


def _load_kv(b, seq, srcs, dsts, sems):
    copies = []
    for n, (src, col, width) in enumerate(srcs):
        cp = pltpu.make_async_copy(src.at[pl.ds(b * seq, seq), pl.ds(col, width)], dsts[n], sems.at[n])
        cp.start()
        copies.append(cp)
    for cp in copies:
        cp.wait()


def _fcum_kernel(ex_ref, bf_ref, o_ref, carry_ref):
    i = pl.program_id(1)

    @pl.when(i == 0)
    def _():
        carry_ref[...] = jnp.zeros_like(carry_ref)

    lf = _log_sigmoid(ex_ref[...] + bf_ref[...])
    F = _cumsum_rows(lf) + carry_ref[...]
    o_ref[...] = F
    carry_ref[...] = F[F.shape[0] - 1:, :]


def _forget_cumsum(extra, b_forget_row, batch, seq, tb=256):
    nblk = seq // tb
    return pl.pallas_call(
        _fcum_kernel,
        grid=(batch, nblk),
        in_specs=[pl.BlockSpec((tb, 128), lambda b, i: (b * nblk + i, 0)),
                  pl.BlockSpec((1, 128), lambda b, i: (0, 0))],
        out_specs=pl.BlockSpec((tb, 128), lambda b, i: (b * nblk + i, 0)),
        out_shape=jax.ShapeDtypeStruct((batch * seq, 128), F32),
        scratch_shapes=[pltpu.VMEM((1, 128), F32)],
        compiler_params=_cparams(("parallel", "arbitrary")),
        name="forget_cumsum",
    )(extra, b_forget_row)


def _fox_kernel(q_ref, fq_ref, fk_ref, kv_hbm, o_ref, k_vm, v_vm, sems, *, seq):
    b = pl.program_id(0)
    i = pl.program_id(1)
    TQ, TK = FOX_TQ, ATT_TK

    @pl.when(i == 0)
    def _():
        _load_kv(b, seq, [(kv_hbm, CD_KC, 512), (kv_hbm, CD_VC, 512)], [k_vm, v_vm], sems)

    nj = (i * TQ + TQ - 1) // TK + 1
    row = i * TQ + lax.broadcasted_iota(jnp.int32, (TQ, TK), 0)
    col0 = lax.broadcasted_iota(jnp.int32, (TQ, TK), 1)
    for p in range(FOX_HEADS // 2):
        q2 = q_ref[:, p * 128:(p + 1) * 128]
        qms = [jnp.where(_half_mask((TQ, 128), half), q2 * QK_SCALE, jnp.zeros_like(q2)) for half in range(2)]
        fqs = [fq_ref[:, 2 * p + half:2 * p + half + 1] for half in range(2)]

        def body(j, carry, diagonal=False, qms=qms, fqs=fqs, p=p):
            k0 = pl.multiple_of(j * TK, TK)
            k2 = k_vm[pl.ds(k0, TK), p * 128:(p + 1) * 128]
            v2 = v_vm[pl.ds(k0, TK), p * 128:(p + 1) * 128]
            out = []
            for half in range(2):
                m, l, acc = carry[half]
                s = _qk(qms[half], k2) + (fqs[half] - fk_ref[2 * p + half, pl.ds(j, 1), :])
                if diagonal:
                    s = jnp.where(col0 + j * TK <= row, s, MASKED)
                out.append(_flash_step(s, v2, m, l, acc))
            return tuple(out)

        init = (jnp.full((TQ, 1), M_INIT, F32), jnp.zeros((TQ, 1), F32), jnp.zeros((TQ, 128), F32))
        carry = lax.fori_loop(0, nj - 1, body, (init, init))
        (_, l0, acc0), (_, l1, acc1) = body(nj - 1, carry, diagonal=True)
        o_ref[:, p * 128:(p + 1) * 128] = jnp.where(_half_mask((TQ, 128), 0), acc0 / l0, acc1 / l1).astype(o_ref.dtype)


def _fox(main, fcum, batch, seq):
    TQ, TK = FOX_TQ, ATT_TK
    nq = seq // TQ
    nk = seq // TK
    n = batch * seq
    fk = fcum[:, :FOX_HEADS].reshape(batch, seq, FOX_HEADS).transpose(0, 2, 1).reshape(batch * FOX_HEADS, nk, TK)
    return pl.pallas_call(
        functools.partial(_fox_kernel, seq=seq),
        grid=(batch, nq),
        in_specs=[pl.BlockSpec((TQ, 512), lambda b, i: (b * nq + i, CD_QC // 512)),
                  pl.BlockSpec((TQ, 128), lambda b, i: (b * nq + i, 0)),
                  pl.BlockSpec((FOX_HEADS, nk, TK), lambda b, i: (b, 0, 0)),
                  pl.BlockSpec(memory_space=pl.ANY)],
        out_specs=pl.BlockSpec((TQ, 512), lambda b, i: (b * nq + i, 0)),
        out_shape=jax.ShapeDtypeStruct((n, 512), BF16),
        scratch_shapes=[pltpu.VMEM((seq, 512), BF16), pltpu.VMEM((seq, 512), BF16),
                        pltpu.SemaphoreType.DMA((2,))],
        compiler_params=_cparams(("arbitrary", "arbitrary"), VMEM_LIMIT),
        name="fox_attention",
    )(main, fcum, fk, main)


INT_MIN = -(2 ** 31)
N_DELTA = 18


def _sortable(x):
    b = pltpu.bitcast(x, jnp.int32)
    return b ^ ((b >> 31) & 0x7FFFFFFF)


def _fold_lanes(x):
    out = x[:, 0:LANES]
    for c in range(1, x.shape[1] // LANES):
        out = out + x[:, c * LANES:(c + 1) * LANES]
    return out


def _dsa_kernel(q_ref, qi_ref, ex_ref, kv_hbm, kid_hbm, tz_hbm, o_ref,
                k_vm, v_vm, kid_vm, tz_vm, key_vm, qm_vm, qim_vm, mask_vm, m_vm, l_vm, acc_vm, cut_vm, sems,
                *, seq, topk):
    b = pl.program_id(0)
    i = pl.program_id(1)
    TQ, TK = ATT_TQ, ATT_TK

    @pl.when(i == 0)
    def _():
        _load_kv(b, seq, [(kv_hbm, AB_KB, 512), (kv_hbm, AB_VB, 512), (kid_hbm, 0, 128)],
                 [k_vm, v_vm, kid_vm], sems)

    @pl.when((i == 0) & (b == 0))
    def _():
        cp = pltpu.make_async_copy(tz_hbm, tz_vm, sems.at[3])
        cp.start()
        cp.wait()

    nj = (i * TQ + TQ - 1) // TK + 1
    row = i * TQ + lax.broadcasted_iota(jnp.int32, (TQ, TK), 0)
    col0 = lax.broadcasted_iota(jnp.int32, (TQ, TK), 1)

    for p in range(4):
        q2 = q_ref[:, p * 128:(p + 1) * 128]
        qi2 = qi_ref[:, p * 128:(p + 1) * 128]
        for half in range(2):
            keep = _half_mask((TQ, 128), half)
            qm_vm[2 * p + half] = jnp.where(keep, q2 * QK_SCALE, jnp.zeros_like(q2))
            qim_vm[2 * p + half] = jnp.where(keep, qi2, jnp.zeros_like(qi2))

    def score_chunk(j, carry):
        k0 = pl.multiple_of(j * TK, TK)
        kd = kid_vm[pl.ds(k0, TK), :].astype(BF16)
        score = jnp.zeros((TQ, TK), F32)
        for h in range(IDX_HEADS):
            s = lax.dot_general(qim_vm[h], kd, (((1,), (1,)), ((), ())), preferred_element_type=F32)
            w = ex_ref[:, EX_WI + h:EX_WI + h + 1]
            score = score + w * jnp.maximum(s, 0.0)
        score = jnp.where(col0 + j * TK <= row, score, -jnp.inf)
        key_vm[j] = _sortable(score)
        return carry

    lax.fori_loop(0, nj, score_chunk, 0)

    RH = 128
    assert seq // LANES <= 256
    ones_b = jnp.ones((LANES, LANES), BF16)

    def count_rows(hit_fn):
        parts = []
        for r0 in range(0, TQ, RH):
            def body(j, acc, r0=r0):
                for cb in range(TK // LANES):
                    blk = key_vm[j, r0:r0 + RH, cb * LANES:(cb + 1) * LANES]
                    acc = acc + jnp.where(hit_fn(blk, j, r0, cb), 1, 0)
                return acc
            parts.append(lax.fori_loop(0, nj, body, jnp.zeros((RH, LANES), jnp.int32)))
        acc = jnp.concatenate(parts, axis=0)
        tot = jnp.dot(acc.astype(F32).astype(BF16), ones_b, preferred_element_type=F32)
        return tot.astype(jnp.int32)

    def count_ge(c):
        return count_rows(lambda blk, j, r0, cb: blk >= c[r0:r0 + RH])

    ans = jnp.where(count_ge(jnp.zeros((TQ, LANES), jnp.int32)) >= topk, 0, INT_MIN).astype(jnp.int32)

    def bit_step(it, ans):
        cand = ans | (jnp.int32(1) << (30 - it))
        return jnp.where(count_ge(cand) >= topk, cand, ans)

    thr_d = lax.fori_loop(0, 31, bit_step, ans)
    thr = thr_d[:, 0:1]

    cut_vm[...] = jnp.full((TQ, 1), seq, jnp.int32)
    n_ge = count_ge(thr_d)

    @pl.when(jnp.max(n_ge) > topk)
    def _():
        need = topk - count_ge(thr_d + 1)
        colh = lax.broadcasted_iota(jnp.int32, (RH, LANES), 1)

        def count_eq_below(x):
            return count_rows(lambda blk, j, r0, cb: (blk == thr_d[r0:r0 + RH])
                              & (colh + (j * TK + cb * LANES) < x[r0:r0 + RH]))

        nbits = max(1, (seq - 1).bit_length())

        def cut_step(it, x):
            cand = x | (jnp.int32(1) << (nbits - 1 - it))
            return jnp.where(count_eq_below(cand) < need, cand, x)

        cut_d = lax.fori_loop(0, nbits, cut_step, jnp.zeros((TQ, LANES), jnp.int32))
        cut_vm[...] = cut_d[:, 0:1]

    cut = cut_vm[...]

    m_vm[...] = jnp.full(m_vm.shape, M_INIT, F32)
    l_vm[...] = jnp.zeros(l_vm.shape, F32)
    acc_vm[...] = jnp.zeros(acc_vm.shape, F32)

    def attend(j, carry):
        k0 = pl.multiple_of(j * TK, TK)
        col = col0 + j * TK
        key = key_vm[j]
        sel = ((key > thr) | ((key == thr) & (col <= cut))) & (col <= row)
        mask_vm[...] = jnp.where(sel, 0.0, MASKED)
        for h in range(DSA_HEADS):
            p = h // 2
            k2 = k_vm[pl.ds(k0, TK), p * 128:(p + 1) * 128]
            v2 = v_vm[pl.ds(k0, TK), p * 128:(p + 1) * 128]
            tiles = []
            for ra in range(TQ // 128):
                rowt = []
                for cb in range(TK // 128):
                    delta = jnp.clip(i * (TQ // 128) + ra - j * (TK // 128) - cb, 0, N_DELTA - 1)
                    rowt.append(tz_vm[delta, h])
                tiles.append(jnp.concatenate(rowt, axis=1))
            bias = jnp.concatenate(tiles, axis=0)
            s = _qk(qm_vm[h], k2) + bias + mask_vm[...]
            m, l, acc = _flash_step(s, v2, m_vm[h], l_vm[h], acc_vm[h])
            m_vm[h] = m
            l_vm[h] = l
            acc_vm[h] = acc
        return carry

    lax.fori_loop(0, nj, attend, 0)

    for p in range(4):
        o0 = acc_vm[2 * p] / l_vm[2 * p]
        o1 = acc_vm[2 * p + 1] / l_vm[2 * p + 1]
        o_ref[:, p * 128:(p + 1) * 128] = jnp.where(_half_mask((TQ, 128), 0), o0, o1).astype(o_ref.dtype)


def _dsa(main, extra, toeplitz, batch, seq):
    TQ, TK = ATT_TQ, ATT_TK
    nq = seq // TQ
    nk = seq // TK
    n = batch * seq
    topk = min(DSA_TOPK_MAX, seq // 4)
    return pl.pallas_call(
        functools.partial(_dsa_kernel, seq=seq, topk=topk),
        grid=(batch, nq),
        in_specs=[pl.BlockSpec((TQ, 512), lambda b, i: (b * nq + i, AB_QB // 512)),
                  pl.BlockSpec((TQ, 512), lambda b, i: (b * nq + i, AB_QI // 512)),
                  pl.BlockSpec((TQ, 128), lambda b, i: (b * nq + i, 1)),
                  pl.BlockSpec(memory_space=pl.ANY),
                  pl.BlockSpec(memory_space=pl.ANY),
                  pl.BlockSpec(memory_space=pl.ANY)],
        out_specs=pl.BlockSpec((TQ, 512), lambda b, i: (b * nq + i, 0)),
        out_shape=jax.ShapeDtypeStruct((n, 512), BF16),
        scratch_shapes=[pltpu.VMEM((seq, 512), BF16), pltpu.VMEM((seq, 512), BF16),
                        pltpu.VMEM((seq, 128), F32),
                        pltpu.VMEM((N_DELTA, DSA_HEADS, 128, 128), F32),
                        pltpu.VMEM((nk, TQ, TK), jnp.int32),
                        pltpu.VMEM((DSA_HEADS, TQ, 128), BF16), pltpu.VMEM((IDX_HEADS, TQ, 128), BF16),
                        pltpu.VMEM((TQ, TK), F32),
                        pltpu.VMEM((DSA_HEADS, TQ, 1), F32), pltpu.VMEM((DSA_HEADS, TQ, 1), F32),
                        pltpu.VMEM((DSA_HEADS, TQ, 128), F32),
                        pltpu.VMEM((TQ, 1), jnp.int32),
                        pltpu.SemaphoreType.DMA((4,))],
        compiler_params=_cparams(("arbitrary", "arbitrary"), VMEM_LIMIT),
        name="dsa_attention",
    )(main, main, extra, main, extra, toeplitz)


DIL_TQ = 128


def _dil_kernel(q_ref, kp_ref, kc_ref, vp_ref, vc_ref, bm_ref, o_ref, ld_ref):
    i = pl.program_id(2)
    TQ = DIL_TQ
    col = lax.broadcasted_iota(jnp.int32, (TQ, 2 * TQ), 1)
    first = jnp.where((i == 0) & (col < TQ), MASKED, 0.0)
    for p in range(DIL_HEADS // 2):
        sl = slice(p * 128, (p + 1) * 128)
        q2 = q_ref[:, sl]
        kc = jnp.concatenate([kp_ref[:, sl], kc_ref[:, sl]], axis=0)
        vc = jnp.concatenate([vp_ref[:, sl], vc_ref[:, sl]], axis=0)
        outs, lds = [], []
        for half in range(2):
            h = 2 * p + half
            qm = jnp.where(_half_mask((TQ, 128), half), q2 * QK_SCALE, jnp.zeros_like(q2))
            lg = _qk(qm, kc) + bm_ref[h] + first
            m = jnp.max(lg, axis=-1, keepdims=True)
            e = jnp.exp(lg - m)
            s = jnp.sum(e, axis=-1, keepdims=True)
            outs.append(jnp.dot(e.astype(BF16), vc, preferred_element_type=F32) / s)
            lds.append(jnp.broadcast_to(m + jnp.log(s), (TQ, 128)))
        lo = _half_mask((TQ, 128), 0)
        o_ref[:, sl] = jnp.where(lo, outs[0], outs[1])
        ld_ref[:, sl] = jnp.where(lo, lds[0], lds[1])


def _dilated_pattern(main, biasmask, dil, batch, seq):
    TQ = DIL_TQ
    n = batch * seq
    ld_seq = seq // dil
    nblk = ld_seq // TQ
    view = main.reshape(n // dil, dil * CD_MAIN)
    cpb = CD_MAIN // 512

    def qmap(b, r, i):
        return (b * nblk + i, r * cpb + CD_QD // 512)

    def kmap(off):
        def f(b, r, i):
            return (b * nblk + jnp.maximum(i - 1 + off, 0), r * cpb + CD_KD // 512)
        return f

    def vmap_(off):
        def f(b, r, i):
            return (b * nblk + jnp.maximum(i - 1 + off, 0), r * cpb + CD_VD // 512)
        return f

    o, ld = pl.pallas_call(
        _dil_kernel,
        grid=(batch, dil, nblk),
        in_specs=[pl.BlockSpec((TQ, 512), qmap),
                  pl.BlockSpec((TQ, 512), kmap(0)), pl.BlockSpec((TQ, 512), kmap(1)),
                  pl.BlockSpec((TQ, 512), vmap_(0)), pl.BlockSpec((TQ, 512), vmap_(1)),
                  pl.BlockSpec((DIL_HEADS, TQ, 2 * TQ), lambda b, r, i: (0, 0, 0))],
        out_specs=[pl.BlockSpec((TQ, 512), lambda b, r, i: (b * nblk + i, r)),
                   pl.BlockSpec((TQ, 512), lambda b, r, i: (b * nblk + i, r))],
        out_shape=[jax.ShapeDtypeStruct((n // dil, dil * 512), F32),
                   jax.ShapeDtypeStruct((n // dil, dil * 512), F32)],
        compiler_params=_cparams(("parallel", "parallel", "parallel"), VMEM_LIMIT),
        name="dilated_attention",
    )(view, view, view, view, view, biasmask)
    return o.reshape(n, 512), ld.reshape(n, 512)


def _dil_merge_kernel(o0, l0, o1, l1, o2, l2, out_ref):
    a, b, c = l0[...], l1[...], l2[...]
    mx = jnp.maximum(jnp.maximum(a, b), c)
    ea, eb, ec = jnp.exp(a - mx), jnp.exp(b - mx), jnp.exp(c - mx)
    tot = ea + eb + ec
    out_ref[...] = ((ea * o0[...] + eb * o1[...] + ec * o2[...]) / tot).astype(out_ref.dtype)


def _dilated(main, bvec, batch, seq):
    n = batch * seq
    parts = []
    a = jnp.arange(DIL_TQ)[:, None]
    c = jnp.arange(2 * DIL_TQ)[None, :]
    off = a - c + DIL_TQ
    inside = (off >= 0) & (off <= DIL_TQ)
    for window, dil in DIL_PATTERNS:
        assert window // dil == DIL_TQ
        bias = bvec[:, BVEC_SHIFT + dil * jnp.clip(off, 0, DIL_TQ)]
        biasmask = jnp.where(inside[None], bias, MASKED)
        parts.extend(_dilated_pattern(main, biasmask, dil, batch, seq))
    tm = 512
    spec = pl.BlockSpec((tm, 512), lambda i: (i, 0))
    return pl.pallas_call(
        _dil_merge_kernel,
        grid=(n // tm,),
        in_specs=[spec] * 6,
        out_specs=spec,
        out_shape=jax.ShapeDtypeStruct((n, 512), BF16),
        compiler_params=_cparams(("parallel",)),
        name="dilated_merge",
    )(*parts)


MOE_TM = 512
MOE_TF = 512


def _split3(a):
    hi = a.astype(BF16)
    r1 = a - hi.astype(F32)
    mid = r1.astype(BF16)
    lo = (r1 - mid.astype(F32)).astype(BF16)
    return hi, mid, lo


def _router_kernel(x_ref, w_ref, idx_ref, gate_ref):
    xh, xm, xl = _split3(x_ref[...])
    wh, wm, wl = _split3(w_ref[...])
    dot = functools.partial(jnp.dot, preferred_element_type=F32)
    logits = (dot(xh, wh) + (dot(xh, wm) + dot(xm, wh))
              + (dot(xh, wl) + dot(xm, wm) + dot(xl, wh)))
    lane = lax.broadcasted_iota(jnp.int32, logits.shape, 1)
    lg = jnp.where(lane < N_EXPERTS, logits, -jnp.inf)
    v1 = jnp.max(lg, axis=-1, keepdims=True)
    i1 = jnp.min(jnp.where(lg == v1, lane, LANES), axis=-1, keepdims=True)
    lg2 = jnp.where(lane == i1, -jnp.inf, lg)
    v2 = jnp.max(lg2, axis=-1, keepdims=True)
    i2 = jnp.min(jnp.where(lg2 == v2, lane, LANES), axis=-1, keepdims=True)
    e2 = jnp.exp(v2 - v1)
    den = 1.0 + e2
    idx_ref[...] = jnp.where(lane == 0, i1, jnp.where(lane == 1, i2, 0))
    gate_ref[...] = jnp.where(lane == 0, 1.0 / den, jnp.where(lane == 1, e2 / den, 0.0))


def _router(x, w_router_pad, tm=512):
    n, d = x.shape
    return pl.pallas_call(
        _router_kernel,
        grid=(n // tm,),
        in_specs=[pl.BlockSpec((tm, d), lambda i: (i, 0)),
                  pl.BlockSpec((d, LANES), lambda i: (0, 0))],
        out_specs=[pl.BlockSpec((tm, LANES), lambda i: (i, 0)),
                   pl.BlockSpec((tm, LANES), lambda i: (i, 0))],
        out_shape=[jax.ShapeDtypeStruct((n, LANES), jnp.int32),
                   jax.ShapeDtypeStruct((n, LANES), F32)],
        compiler_params=_cparams(("parallel",), VMEM_LIMIT),
        name="moe_router",
    )(x, w_router_pad)


MOE_TC = 512
MOE_WIN = 640


def _moe_ffn_kernel(te_ref, nv_ref, clo_ref, chi_ref, src_ref, x_hbm, w1_ref, w3_ref, w2_ref, gate_ref,
                    hi_ref, lo_ref, acc_ref, xs_ref, xbuf, sems):
    t = pl.program_id(0)
    f = pl.program_id(1)
    last = pl.num_programs(1) - 1
    live = t < nv_ref[0]
    tm = acc_ref.shape[0]

    def fetch(c, slot):
        return pltpu.make_async_copy(x_hbm.at[pl.ds(c * MOE_TC, MOE_TC)], xbuf.at[slot], sems.at[slot])

    @pl.when(live & (f == 0))
    def _():
        clo = clo_ref[t]
        chi = chi_ref[t]
        acc_ref[...] = jnp.zeros_like(acc_ref)
        fetch(clo, 0).start()
        src = src_ref[...]
        lane = lax.broadcasted_iota(jnp.int32, (tm, MOE_TC), 1)

        def body(c, carry):
            slot = (c - clo) & 1

            @pl.when(c < chi)
            def _():
                fetch(c + 1, 1 - slot).start()

            fetch(c, slot).wait()
            onehot = jnp.where(src == lane + c * MOE_TC, 1.0, 0.0).astype(BF16)
            acc_ref[...] += jnp.dot(onehot, xbuf[slot], preferred_element_type=F32)
            return carry

        lax.fori_loop(clo, chi + 1, body, 0)
        xs_ref[...] = acc_ref[...].astype(BF16)

    @pl.when(f == 0)
    def _():
        acc_ref[...] = jnp.zeros_like(acc_ref)

    @pl.when(live)
    def _():
        xb = xs_ref[...]
        a = jnp.dot(xb, w1_ref[...], preferred_element_type=F32)
        c = jnp.dot(xb, w3_ref[...], preferred_element_type=F32)
        h = (_silu(a) * c).astype(BF16)
        acc_ref[...] += jnp.dot(h, w2_ref[...], preferred_element_type=F32)

    @pl.when(f == last)
    def _():
        y = acc_ref[...] * gate_ref[...]
        hi = y.astype(BF16)
        hi_ref[...] = hi
        lo_ref[...] = (y - hi.astype(F32)).astype(BF16)


def _moe_ffn(xb, row_src, w1, w3, w2, gate_sorted, tile_expert, n_live, chunk_lo, chunk_hi):
    p_rows = row_src.shape[0]
    d = xb.shape[1]
    ff = w1.shape[2]
    tm, tf = MOE_TM, MOE_TF

    def rows(t, f, te, nv, clo, chi):
        return (t, 0)

    return pl.pallas_call(
        _moe_ffn_kernel,
        grid_spec=pltpu.PrefetchScalarGridSpec(
            num_scalar_prefetch=4,
            grid=(p_rows // tm, ff // tf),
            in_specs=[pl.BlockSpec((tm, 1), rows),
                      pl.BlockSpec(memory_space=pl.ANY),
                      pl.BlockSpec((None, d, tf), lambda t, f, te, nv, clo, chi: (te[t], 0, f)),
                      pl.BlockSpec((None, d, tf), lambda t, f, te, nv, clo, chi: (te[t], 0, f)),
                      pl.BlockSpec((None, tf, d), lambda t, f, te, nv, clo, chi: (te[t], f, 0)),
                      pl.BlockSpec((tm, 1), rows)],
            out_specs=[pl.BlockSpec((tm, d), rows), pl.BlockSpec((tm, d), rows)],
            scratch_shapes=[pltpu.VMEM((tm, d), F32), pltpu.VMEM((tm, d), BF16),
                            pltpu.VMEM((2, MOE_TC, d), BF16), pltpu.SemaphoreType.DMA((2,))],
        ),
        out_shape=[jax.ShapeDtypeStruct((p_rows, d), BF16), jax.ShapeDtypeStruct((p_rows, d), BF16)],
        compiler_params=_cparams(("arbitrary", "arbitrary"), VMEM_LIMIT),
        name="moe_grouped_swiglu",
    )(tile_expert, n_live, chunk_lo, chunk_hi, row_src, xb, w1, w3, w2, gate_sorted)


def _combine_ln_kernel(win_ref, pos_ref, eid_ref, hi_hbm, lo_hbm, x_ref, g_ref, b_ref, xf_ref, xb_ref,
                       acc_ref, hbuf, lbuf, sems):
    blk = pl.program_id(0)
    tm = acc_ref.shape[0]

    def fetch(e, slot):
        w = pl.multiple_of(win_ref[blk * N_EXPERTS + e], 16)
        return (pltpu.make_async_copy(hi_hbm.at[pl.ds(w, MOE_WIN)], hbuf.at[slot], sems.at[0, slot]),
                pltpu.make_async_copy(lo_hbm.at[pl.ds(w, MOE_WIN)], lbuf.at[slot], sems.at[1, slot]))

    for cp in fetch(0, 0):
        cp.start()
    pos0 = pos_ref[:, 0:1]
    pos1 = pos_ref[:, 1:2]
    lane = lax.broadcasted_iota(jnp.int32, (tm, MOE_WIN), 1)
    acc_ref[...] = ALPHA * x_ref[...]
    for e in range(N_EXPERTS):
        slot = e & 1
        if e + 1 < N_EXPERTS:
            for cp in fetch(e + 1, 1 - slot):
                cp.start()
        for cp in fetch(e, slot):
            cp.wait()
        w = win_ref[blk * N_EXPERTS + e]
        rel0 = jnp.where(eid_ref[:, 0:1] == e, pos0 - w, -1)
        rel1 = jnp.where(eid_ref[:, 1:2] == e, pos1 - w, -1)
        onehot = jnp.where((lane == rel0) | (lane == rel1), 1.0, 0.0).astype(BF16)
        acc_ref[...] += (jnp.dot(onehot, hbuf[slot], preferred_element_type=F32)
                         + jnp.dot(onehot, lbuf[slot], preferred_element_type=F32))
    y = _layernorm_rows(acc_ref[...], g_ref[...], b_ref[...])
    xf_ref[...] = y
    xb_ref[...] = y.astype(BF16)


def _combine_ln(win, pos2, eid2, ys_hi, ys_lo, x, g, b):
    n, d = x.shape
    tm = MOE_TC
    return pl.pallas_call(
        _combine_ln_kernel,
        grid_spec=pltpu.PrefetchScalarGridSpec(
            num_scalar_prefetch=1,
            grid=(n // tm,),
            in_specs=[pl.BlockSpec((tm, TOP_K), lambda i, w: (i, 0)),
                      pl.BlockSpec((tm, TOP_K), lambda i, w: (i, 0)),
                      pl.BlockSpec(memory_space=pl.ANY),
                      pl.BlockSpec(memory_space=pl.ANY),
                      pl.BlockSpec((tm, d), lambda i, w: (i, 0)),
                      pl.BlockSpec((1, d), lambda i, w: (0, 0)),
                      pl.BlockSpec((1, d), lambda i, w: (0, 0))],
            out_specs=[pl.BlockSpec((tm, d), lambda i, w: (i, 0)),
                       pl.BlockSpec((tm, d), lambda i, w: (i, 0))],
            scratch_shapes=[pltpu.VMEM((tm, d), F32),
                            pltpu.VMEM((2, MOE_WIN, d), BF16), pltpu.VMEM((2, MOE_WIN, d), BF16),
                            pltpu.SemaphoreType.DMA((2, 2))],
        ),
        out_shape=[jax.ShapeDtypeStruct((n, d), F32), jax.ShapeDtypeStruct((n, d), BF16)],
        compiler_params=_cparams(("arbitrary",), VMEM_LIMIT),
        name="moe_combine_ln",
    )(win, pos2, eid2, ys_hi, ys_lo, x, g, b)


def _moe(x, xb, w_router_pad, w1, w3, w2, g, b):
    n, d = x.shape
    tm = MOE_TM
    idx, gates = _router(x, w_router_pad)
    e_flat = idx[:, :TOP_K].reshape(-1)
    g_flat = gates[:, :TOP_K].reshape(-1)
    onehot = (e_flat[:, None] == jnp.arange(N_EXPERTS, dtype=jnp.int32)[None, :]).astype(jnp.int32)
    csum = jnp.cumsum(onehot, axis=0)
    rank = jnp.take_along_axis(csum, e_flat[:, None], axis=1)[:, 0] - 1
    counts = csum[-1]
    padded = ((counts + tm - 1) // tm) * tm
    ends = jnp.cumsum(padded)
    pos = ((ends - padded)[e_flat] + rank).astype(jnp.int32)
    tiles = (TOP_K * n) // tm + N_EXPERTS + 2
    p_rows = tiles * tm
    token = jnp.arange(TOP_K * n, dtype=jnp.int32) // TOP_K
    row_src = jnp.full((p_rows,), -1, jnp.int32).at[pos].set(token)
    gate_sorted = jnp.zeros((p_rows,), F32).at[pos].set(g_flat).reshape(p_rows, 1)
    tile_start = jnp.arange(tiles, dtype=jnp.int32) * tm
    tile_expert = jnp.minimum(jnp.searchsorted(ends, tile_start, side="right"), N_EXPERTS - 1).astype(jnp.int32)
    n_live = (ends[-1] // tm).astype(jnp.int32).reshape(1)
    per_tile = row_src.reshape(tiles, tm)
    chunk_lo = (jnp.min(jnp.where(per_tile >= 0, per_tile, n - 1), axis=1) // MOE_TC).astype(jnp.int32)
    chunk_hi = (jnp.max(per_tile, axis=1) // MOE_TC).astype(jnp.int32)
    big = jnp.int32(p_rows)
    wkey = (token // MOE_TC) * N_EXPERTS + e_flat
    win = jnp.full(((n // MOE_TC) * N_EXPERTS,), big, jnp.int32).at[wkey].min(pos)
    win = (jnp.where(win == big, 0, win) // 16) * 16

    ys_hi, ys_lo = _moe_ffn(xb, row_src.reshape(p_rows, 1), w1, w3, w2, gate_sorted, tile_expert, n_live,
                            chunk_lo, chunk_hi)
    return _combine_ln(win, pos.reshape(n, TOP_K), idx[:, :TOP_K], ys_hi, ys_lo, x, g, b)


def _regroup_ab(w):
    main = jnp.concatenate([w[:, 0:1536], w[:, 1552:3600]], axis=1)
    ki = w[:, 3600:3664]
    extra = jnp.concatenate([ki, ki, w[:, 1536:1552], w[:, 3664:3672],
                             jnp.zeros((w.shape[0], 128 - GLA_GATE_RANK - IDX_HEADS), w.dtype)], axis=1)
    return main.astype(BF16), extra.astype(BF16)


def _regroup_cd(w):
    main = jnp.concatenate([w[:, 0:1536], w[:, 1544:3080]], axis=1)
    extra = jnp.concatenate([w[:, 1536:1544], jnp.zeros((w.shape[0], 128 - FOX_HEADS), w.dtype)], axis=1)
    return main.astype(BF16), extra.astype(BF16)


def _toeplitz_tiles(bvec):
    t = 128
    seg = jnp.stack([bvec[:, dl * t:dl * t + 2 * t - 1] for dl in range(N_DELTA - 1)], axis=1)
    w = jnp.flip(seg, axis=-1)
    w = jnp.concatenate([w, w[..., :1]], axis=-1)
    skew = jnp.tile(w, (1, 1, t))[..., :t * (2 * t - 1)].reshape(REL_HEADS, N_DELTA - 1, t, 2 * t - 1)
    near = skew[..., t - 1:]
    far = jnp.broadcast_to(bvec[:, BVEC_LEN - 1][:, None, None, None], (REL_HEADS, 1, t, t))
    return jnp.concatenate([near, far], axis=1).transpose(1, 0, 2, 3)


def kernel(x, ln_g, ln_b, rel_table, w_in_ab, w_gate_a, b_gate_a, g_norm_a, w_out_ab, w_in_cd, b_forget,
           w_out_cd, w1_dense, w3_dense, w2_dense, w_router, w1_moe, w3_moe, w2_moe):
    batch, seq, d = x.shape
    n = batch * seq
    xf = x.reshape(n, d)
    xb = xf.astype(BF16)

    bvec = _bias_vector(rel_table)
    toeplitz = _toeplitz_tiles(bvec)

    for layer in range(DEPTH):
        j = layer // 2
        g0, b0 = ln_g[layer, 0][None, :], ln_b[layer, 0][None, :]
        g1, b1 = ln_g[layer, 1][None, :], ln_b[layer, 1][None, :]
        if layer % 2 == 0:
            w_main, w_extra = _regroup_ab(w_in_ab[j])
            main = _matmul(xb, w_main, BF16, 512, 512)
            extra = _matmul(xb, w_extra, F32, 512, AB_EXTRA)
            oa = _gla(main, extra, w_gate_a[j], b_gate_a[j][None, :], g_norm_a[j][None, :], batch, seq)
            ob = _dsa(main, extra, toeplitz, batch, seq)
            o = jnp.concatenate([oa, ob], axis=1)
            xf, xb = _outproj_ln(o, w_out_ab[j].astype(BF16), xf, g0, b0)
            xf, xb = _swiglu_ln(xb, w1_dense[j].astype(BF16), w3_dense[j].astype(BF16),
                                w2_dense[j].astype(BF16), xf, g1, b1)
        else:
            w_main, w_extra = _regroup_cd(w_in_cd[j])
            main = _matmul(xb, w_main, BF16, 512, 512)
            extra = _matmul(xb, w_extra, F32, 512, CD_EXTRA)
            bf_row = jnp.concatenate([b_forget[j], jnp.zeros((128 - FOX_HEADS,), F32)])[None, :]
            fcum = _forget_cumsum(extra, bf_row, batch, seq)
            oc = _fox(main, fcum, batch, seq)
            od = _dilated(main, bvec, batch, seq)
            o = jnp.concatenate([oc, od], axis=1)
            xf, xb = _outproj_ln(o, w_out_cd[j].astype(BF16), xf, g0, b0)
            w_r = jnp.concatenate([w_router[j], jnp.zeros((d, LANES - N_EXPERTS), F32)], axis=1)
            xf, xb = _moe(xf, xb, w_r, w1_moe[j].astype(BF16), w3_moe[j].astype(BF16),
                          w2_moe[j].astype(BF16), g1, b1)
    return xf.reshape(batch, seq, d)
```

```python
import functools
import math

import jax
import jax.numpy as jnp
from jax import lax
from jax.experimental import pallas as pl
from jax.experimental.pallas import tpu as pltpu

D_MODEL = 1024
DEPTH = 4
HEAD_DIM = 64
GLA_HEADS = 4
GLA_DK = 64
GLA_DV = 128
GLA_GATE_RANK = 16
GLA_GATE_TAU = 16.0
GLA_CHUNK = 64
DSA_HEADS = 8
IDX_HEADS = 8
IDX_DIM = 64
DSA_TOPK_MAX = 256
FOX_HEADS = 8
DIL_HEADS = 8
DIL_PATTERNS = ((128, 1), (512, 4), (2048, 16))
REL_BUCKETS = 32
REL_MAX_DIST = 2048
REL_HEADS = 8
D_FF = 2816
N_EXPERTS = 8
TOP_K = 2
D_FF_EXPERT = 3584
ALPHA = (2 * DEPTH) ** 0.25
LN_EPS = 1e-5

LANES = 128
MASKED = -1e30
M_INIT = -1e20
VMEM_LIMIT = 56 * 1024 * 1024

F32 = jnp.float32
BF16 = jnp.bfloat16

AB_QA, AB_KA, AB_VA, AB_RA, AB_QB, AB_KB, AB_VB, AB_QI = 0, 256, 512, 1024, 1536, 2048, 2560, 3072
AB_MAIN = 3584
AB_EXTRA = 256
EX_GA = 0
EX_WI = 16
CD_QC, CD_KC, CD_VC, CD_QD, CD_KD, CD_VD = 0, 512, 1024, 1536, 2048, 2560
CD_MAIN = 3072
CD_EXTRA = 128


def _cparams(sem, vmem=None):
    return pltpu.CompilerParams(dimension_semantics=sem, vmem_limit_bytes=vmem)


def _mm_kernel(x_ref, w_ref, o_ref):
    o_ref[...] = jnp.dot(x_ref[...], w_ref[...], preferred_element_type=F32).astype(o_ref.dtype)


def _matmul(x, w, out_dtype, tm, tn):
    n, k = x.shape
    m = w.shape[1]
    return pl.pallas_call(
        _mm_kernel,
        grid=(n // tm, m // tn),
        in_specs=[pl.BlockSpec((tm, k), lambda i, j: (i, 0)),
                  pl.BlockSpec((k, tn), lambda i, j: (0, j))],
        out_specs=pl.BlockSpec((tm, tn), lambda i, j: (i, j)),
        out_shape=jax.ShapeDtypeStruct((n, m), out_dtype),
        compiler_params=_cparams(("parallel", "parallel"), VMEM_LIMIT),
        name="proj_matmul",
    )(x, w)


def _layernorm_rows(z, g, b):
    mu = jnp.mean(z, axis=-1, keepdims=True)
    zc = z - mu
    var = jnp.mean(zc * zc, axis=-1, keepdims=True)
    return zc * lax.rsqrt(var + LN_EPS) * g + b


def _outproj_ln_kernel(o_ref, w_ref, x_ref, g_ref, b_ref, xf_ref, xb_ref):
    mix = jnp.dot(o_ref[...], w_ref[...], preferred_element_type=F32)
    y = _layernorm_rows(ALPHA * x_ref[...] + mix, g_ref[...], b_ref[...])
    xf_ref[...] = y
    xb_ref[...] = y.astype(BF16)


def _outproj_ln(o, w, x, g, b, tm=512):
    n, k = o.shape
    d = w.shape[1]
    return pl.pallas_call(
        _outproj_ln_kernel,
        grid=(n // tm,),
        in_specs=[pl.BlockSpec((tm, k), lambda i: (i, 0)),
                  pl.BlockSpec((k, d), lambda i: (0, 0)),
                  pl.BlockSpec((tm, d), lambda i: (i, 0)),
                  pl.BlockSpec((1, d), lambda i: (0, 0)),
                  pl.BlockSpec((1, d), lambda i: (0, 0))],
        out_specs=[pl.BlockSpec((tm, d), lambda i: (i, 0)),
                   pl.BlockSpec((tm, d), lambda i: (i, 0))],
        out_shape=[jax.ShapeDtypeStruct((n, d), F32), jax.ShapeDtypeStruct((n, d), BF16)],
        compiler_params=_cparams(("parallel",), VMEM_LIMIT),
        name="outproj_ln",
    )(o, w, x, g, b)


def _silu(a):
    return a / (1.0 + jnp.exp(-a))


def _swiglu_ln_kernel(xb_ref, w1_ref, w3_ref, w2_ref, x_ref, g_ref, b_ref, xf_ref, xbo_ref, acc_ref):
    f = pl.program_id(1)

    @pl.when(f == 0)
    def _():
        acc_ref[...] = jnp.zeros_like(acc_ref)

    xb = xb_ref[...]
    a = jnp.dot(xb, w1_ref[...], preferred_element_type=F32)
    c = jnp.dot(xb, w3_ref[...], preferred_element_type=F32)
    h = (_silu(a) * c).astype(BF16)
    acc_ref[...] += jnp.dot(h, w2_ref[...], preferred_element_type=F32)

    @pl.when(f == pl.num_programs(1) - 1)
    def _():
        y = _layernorm_rows(ALPHA * x_ref[...] + acc_ref[...], g_ref[...], b_ref[...])
        xf_ref[...] = y
        xbo_ref[...] = y.astype(BF16)


def _swiglu_ln(xb, w1, w3, w2, x, g, b, tm=512, tf=1408):
    n, d = xb.shape
    ff = w1.shape[1]
    return pl.pallas_call(
        _swiglu_ln_kernel,
        grid=(n // tm, ff // tf),
        in_specs=[pl.BlockSpec((tm, d), lambda i, f: (i, 0)),
                  pl.BlockSpec((d, tf), lambda i, f: (0, f)),
                  pl.BlockSpec((d, tf), lambda i, f: (0, f)),
                  pl.BlockSpec((tf, d), lambda i, f: (f, 0)),
                  pl.BlockSpec((tm, d), lambda i, f: (i, 0)),
                  pl.BlockSpec((1, d), lambda i, f: (0, 0)),
                  pl.BlockSpec((1, d), lambda i, f: (0, 0))],
        out_specs=[pl.BlockSpec((tm, d), lambda i, f: (i, 0)),
                   pl.BlockSpec((tm, d), lambda i, f: (i, 0))],
        out_shape=[jax.ShapeDtypeStruct((n, d), F32), jax.ShapeDtypeStruct((n, d), BF16)],
        scratch_shapes=[pltpu.VMEM((tm, d), F32)],
        compiler_params=_cparams(("parallel", "arbitrary"), VMEM_LIMIT),
        name="swiglu_ln",
    )(xb, w1, w3, w2, x, g, b)


BVEC_LEN = 2304
BVEC_SHIFT = 127


def _bvec_kernel(tab_ref, o_ref):
    n = lax.broadcasted_iota(jnp.int32, (REL_HEADS, BVEC_LEN), 1)
    d = jnp.maximum(n - BVEC_SHIFT, 0)
    max_exact = REL_BUCKETS // 2
    df = jnp.maximum(d, 1).astype(F32)
    large = max_exact + (jnp.log(df / max_exact) / math.log(REL_MAX_DIST / max_exact)
                         * (REL_BUCKETS - max_exact)).astype(jnp.int32)
    large = jnp.minimum(large, REL_BUCKETS - 1)
    bucket = jnp.where(d < max_exact, d, large)
    acc = jnp.zeros((REL_HEADS, BVEC_LEN), F32)
    for k in range(REL_BUCKETS):
        acc = jnp.where(bucket == k, tab_ref[:, k:k + 1], acc)
    o_ref[...] = acc


def _bias_vector(rel_table):
    return pl.pallas_call(
        _bvec_kernel,
        out_shape=jax.ShapeDtypeStruct((REL_HEADS, BVEC_LEN), F32),
        name="t5_bias_vector",
    )(rel_table.T)


GLA_TB = 256


def _log_sigmoid(z):
    return jnp.minimum(z, 0.0) - jnp.log(1.0 + jnp.exp(-jnp.abs(z)))


def _cumsum_rows(x):
    rows = x.shape[0]
    row = lax.broadcasted_iota(jnp.int32, x.shape, 0)
    sh = 1
    while sh < rows:
        x = x + jnp.where(row >= sh, pltpu.roll(x, sh, 0), 0.0)
        sh *= 2
    return x


def _gla_kernel(q_ref, k_ref, v_ref, r_ref, ga_ref, wg_ref, bg_ref, gn_ref, o_ref, s_ref, a_ref):
    c_id = pl.program_id(2)

    @pl.when(c_id == 0)
    def _():
        s_ref[...] = jnp.zeros_like(s_ref)

    C = GLA_CHUNK
    srow = lax.broadcasted_iota(jnp.int32, (2 * GLA_DK, 2 * GLA_DV), 0)
    scol = lax.broadcasted_iota(jnp.int32, (2 * GLA_DK, 2 * GLA_DV), 1)
    blockdiag = (srow < GLA_DK) == (scol < GLA_DV)
    trow = lax.broadcasted_iota(jnp.int32, (C, C), 0)
    tcol = lax.broadcasted_iota(jnp.int32, (C, C), 1)
    causal = tcol <= trow

    def chunk(c, carry):
        r0 = pl.multiple_of(c * C, C)
        q2 = q_ref[pl.ds(r0, C), :].astype(F32) * (GLA_DK ** -0.5)
        k2 = k_ref[pl.ds(r0, C), :].astype(F32)
        v2 = v_ref[pl.ds(r0, C), :]
        ga = ga_ref[pl.ds(r0, C), :][:, EX_GA:EX_GA + GLA_GATE_RANK]
        z = jnp.dot(ga.astype(BF16), wg_ref[...].astype(BF16), preferred_element_type=F32) + bg_ref[...]
        g = _log_sigmoid(z) / GLA_GATE_TAU
        G = _cumsum_rows(g)
        GT = G.T
        kT = k2.T
        qT = q2.T
        for t in range(C):
            dec = jnp.exp(GT[:, t:t + 1] - GT)
            prod = (qT[:, t:t + 1] * kT) * dec
            a_ref[0, t:t + 1, :] = jnp.sum(prod[:GLA_DK], axis=0, keepdims=True)
            a_ref[1, t:t + 1, :] = jnp.sum(prod[GLA_DK:], axis=0, keepdims=True)
        S = s_ref[...]
        o_inter = jnp.dot((q2 * jnp.exp(G)).astype(BF16), S.astype(BF16), preferred_element_type=F32)
        o_intra = []
        for h in range(2):
            A = jnp.where(causal, a_ref[h], 0.0)
            o_intra.append(jnp.dot(A.astype(BF16), v2[:, h * GLA_DV:(h + 1) * GLA_DV],
                                   preferred_element_type=F32))
        o2 = o_inter + jnp.concatenate(o_intra, axis=1)
        g_last = GT[:, C - 1:C]
        kd = kT * jnp.exp(g_last - GT)
        upd = jnp.dot(kd.astype(BF16), v2, preferred_element_type=F32)
        s_ref[...] = jnp.exp(g_last) * S + jnp.where(blockdiag, upd, 0.0)
        outs = []
        for h in range(2):
            of = o2[:, h * GLA_DV:(h + 1) * GLA_DV]
            of = of * lax.rsqrt(jnp.mean(of * of, axis=-1, keepdims=True) + LN_EPS) * gn_ref[...]
            rr = r_ref[pl.ds(r0, C), h * GLA_DV:(h + 1) * GLA_DV].astype(F32)
            outs.append(of * _silu(rr))
        o_ref[pl.ds(r0, C), :] = jnp.concatenate(outs, axis=1).astype(o_ref.dtype)
        return carry

    lax.fori_loop(0, GLA_TB // C, chunk, 0)


def _gla(main, extra, w_gate, b_gate, g_norm, batch, seq):
    nblk = seq // GLA_TB
    n = batch * seq

    def rows(b, hp, c):
        return b * nblk + c

    return pl.pallas_call(
        _gla_kernel,
        grid=(batch, 2, nblk),
        in_specs=[
            pl.BlockSpec((GLA_TB, 128), lambda b, hp, c: (rows(b, hp, c), AB_QA // 128 + hp)),
            pl.BlockSpec((GLA_TB, 128), lambda b, hp, c: (rows(b, hp, c), AB_KA // 128 + hp)),
            pl.BlockSpec((GLA_TB, 256), lambda b, hp, c: (rows(b, hp, c), AB_VA // 256 + hp)),
            pl.BlockSpec((GLA_TB, 256), lambda b, hp, c: (rows(b, hp, c), AB_RA // 256 + hp)),
            pl.BlockSpec((GLA_TB, 128), lambda b, hp, c: (rows(b, hp, c), 1)),
            pl.BlockSpec((GLA_GATE_RANK, 128), lambda b, hp, c: (0, hp)),
            pl.BlockSpec((1, 128), lambda b, hp, c: (0, hp)),
            pl.BlockSpec((1, GLA_DV), lambda b, hp, c: (0, 0)),
        ],
        out_specs=pl.BlockSpec((GLA_TB, 256), lambda b, hp, c: (rows(b, hp, c), hp)),
        out_shape=jax.ShapeDtypeStruct((n, GLA_HEADS * GLA_DV), BF16),
        scratch_shapes=[pltpu.VMEM((2 * GLA_DK, 2 * GLA_DV), F32),
                        pltpu.VMEM((2, GLA_CHUNK, GLA_CHUNK), F32)],
        compiler_params=_cparams(("parallel", "parallel", "arbitrary"), VMEM_LIMIT),
        name="gla",
    )(main, main, main, main, extra, w_gate, b_gate, g_norm)


ATT_TQ = 256
ATT_TK = 1024
FOX_TQ = 512


def _half_mask(shape, half):
    lane = lax.broadcasted_iota(jnp.int32, shape, len(shape) - 1)
    return (lane < HEAD_DIM) if half == 0 else (lane >= HEAD_DIM)


def _flash_step(s, v2, m, l, acc):
    m_new = jnp.maximum(m, jnp.max(s, axis=-1, keepdims=True))
    alpha = jnp.exp(m - m_new)
    p = jnp.exp(s - m_new)
    l = alpha * l + jnp.sum(p, axis=-1, keepdims=True)
    acc = alpha * acc + jnp.dot(p.astype(BF16), v2, preferred_element_type=F32)
    return m_new, l, acc


QK_SCALE = HEAD_DIM ** -0.5


def _qk(qm, k2):
    return lax.dot_general(qm, k2, (((1,), (1,)), ((), ())), preferred_element_type=F32)


def _load_kv(b, seq, srcs, dsts, sems):
    copies = []
    for n, (src, col, width) in enumerate(srcs):
        cp = pltpu.make_async_copy(src.at[pl.ds(b * seq, seq), pl.ds(col, width)], dsts[n], sems.at[n])
        cp.start()
        copies.append(cp)
    for cp in copies:
        cp.wait()


def _fcum_kernel(ex_ref, bf_ref, o_ref, carry_ref):
    i = pl.program_id(1)

    @pl.when(i == 0)
    def _():
        carry_ref[...] = jnp.zeros_like(carry_ref)

    lf = _log_sigmoid(ex_ref[...] + bf_ref[...])
    F = _cumsum_rows(lf) + carry_ref[...]
    o_ref[...] = F
    carry_ref[...] = F[F.shape[0] - 1:, :]


def _forget_cumsum(extra, b_forget_row, batch, seq, tb=256):
    nblk = seq // tb
    return pl.pallas_call(
        _fcum_kernel,
        grid=(batch, nblk),
        in_specs=[pl.BlockSpec((tb, 128), lambda b, i: (b * nblk + i, 0)),
                  pl.BlockSpec((1, 128), lambda b, i: (0, 0))],
        out_specs=pl.BlockSpec((tb, 128), lambda b, i: (b * nblk + i, 0)),
        out_shape=jax.ShapeDtypeStruct((batch * seq, 128), F32),
        scratch_shapes=[pltpu.VMEM((1, 128), F32)],
        compiler_params=_cparams(("parallel", "arbitrary")),
        name="forget_cumsum",
    )(extra, b_forget_row)


def _fox_kernel(q_ref, fq_ref, fk_ref, kv_hbm, o_ref, k_vm, v_vm, sems, *, seq):
    b = pl.program_id(0)
    i = pl.program_id(1)
    TQ, TK = FOX_TQ, ATT_TK

    @pl.when(i == 0)
    def _():
        _load_kv(b, seq, [(kv_hbm, CD_KC, 512), (kv_hbm, CD_VC, 512)], [k_vm, v_vm], sems)

    nj = (i * TQ + TQ - 1) // TK + 1
    row = i * TQ + lax.broadcasted_iota(jnp.int32, (TQ, TK), 0)
    col0 = lax.broadcasted_iota(jnp.int32, (TQ, TK), 1)
    for p in range(FOX_HEADS // 2):
        q2 = q_ref[:, p * 128:(p + 1) * 128]
        qms = [jnp.where(_half_mask((TQ, 128), half), q2 * QK_SCALE, jnp.zeros_like(q2)) for half in range(2)]
        fqs = [fq_ref[:, 2 * p + half:2 * p + half + 1] for half in range(2)]

        def body(j, carry, diagonal=False, qms=qms, fqs=fqs, p=p):
            k0 = pl.multiple_of(j * TK, TK)
            k2 = k_vm[pl.ds(k0, TK), p * 128:(p + 1) * 128]
            v2 = v_vm[pl.ds(k0, TK), p * 128:(p + 1) * 128]
            out = []
            for half in range(2):
                m, l, acc = carry[half]
                s = _qk(qms[half], k2) + (fqs[half] - fk_ref[2 * p + half, pl.ds(j, 1), :])
                if diagonal:
                    s = jnp.where(col0 + j * TK <= row, s, MASKED)
                out.append(_flash_step(s, v2, m, l, acc))
            return tuple(out)

        init = (jnp.full((TQ, 1), M_INIT, F32), jnp.zeros((TQ, 1), F32), jnp.zeros((TQ, 128), F32))
        carry = lax.fori_loop(0, nj - 1, body, (init, init))
        (_, l0, acc0), (_, l1, acc1) = body(nj - 1, carry, diagonal=True)
        o_ref[:, p * 128:(p + 1) * 128] = jnp.where(_half_mask((TQ, 128), 0), acc0 / l0, acc1 / l1).astype(o_ref.dtype)


def _fox(main, fcum, batch, seq):
    TQ, TK = FOX_TQ, ATT_TK
    nq = seq // TQ
    nk = seq // TK
    n = batch * seq
    fk = fcum[:, :FOX_HEADS].reshape(batch, seq, FOX_HEADS).transpose(0, 2, 1).reshape(batch * FOX_HEADS, nk, TK)
    return pl.pallas_call(
        functools.partial(_fox_kernel, seq=seq),
        grid=(batch, nq),
        in_specs=[pl.BlockSpec((TQ, 512), lambda b, i: (b * nq + i, CD_QC // 512)),
                  pl.BlockSpec((TQ, 128), lambda b, i: (b * nq + i, 0)),
                  pl.BlockSpec((FOX_HEADS, nk, TK), lambda b, i: (b, 0, 0)),
                  pl.BlockSpec(memory_space=pl.ANY)],
        out_specs=pl.BlockSpec((TQ, 512), lambda b, i: (b * nq + i, 0)),
        out_shape=jax.ShapeDtypeStruct((n, 512), BF16),
        scratch_shapes=[pltpu.VMEM((seq, 512), BF16), pltpu.VMEM((seq, 512), BF16),
                        pltpu.SemaphoreType.DMA((2,))],
        compiler_params=_cparams(("arbitrary", "arbitrary"), VMEM_LIMIT),
        name="fox_attention",
    )(main, fcum, fk, main)


INT_MIN = -(2 ** 31)
N_DELTA = 18


def _sortable(x):
    b = pltpu.bitcast(x, jnp.int32)
    return b ^ ((b >> 31) & 0x7FFFFFFF)


def _fold_lanes(x):
    out = x[:, 0:LANES]
    for c in range(1, x.shape[1] // LANES):
        out = out + x[:, c * LANES:(c + 1) * LANES]
    return out


def _dsa_kernel(q_ref, qi_ref, ex_ref, kv_hbm, kid_hbm, tz_hbm, o_ref,
                k_vm, v_vm, kid_vm, tz_vm, key_vm, qm_vm, qim_vm, mask_vm, m_vm, l_vm, acc_vm, cut_vm, sems,
                *, seq, topk):
    b = pl.program_id(0)
    i = pl.program_id(1)
    TQ, TK = ATT_TQ, ATT_TK

    @pl.when(i == 0)
    def _():
        _load_kv(b, seq, [(kv_hbm, AB_KB, 512), (kv_hbm, AB_VB, 512), (kid_hbm, 0, 128)],
                 [k_vm, v_vm, kid_vm], sems)

    @pl.when((i == 0) & (b == 0))
    def _():
        cp = pltpu.make_async_copy(tz_hbm, tz_vm, sems.at[3])
        cp.start()
        cp.wait()

    nj = (i * TQ + TQ - 1) // TK + 1
    row = i * TQ + lax.broadcasted_iota(jnp.int32, (TQ, TK), 0)
    col0 = lax.broadcasted_iota(jnp.int32, (TQ, TK), 1)

    for p in range(4):
        q2 = q_ref[:, p * 128:(p + 1) * 128]
        qi2 = qi_ref[:, p * 128:(p + 1) * 128]
        for half in range(2):
            keep = _half_mask((TQ, 128), half)
            qm_vm[2 * p + half] = jnp.where(keep, q2 * QK_SCALE, jnp.zeros_like(q2))
            qim_vm[2 * p + half] = jnp.where(keep, qi2, jnp.zeros_like(qi2))

    def score_chunk(j, carry):
        k0 = pl.multiple_of(j * TK, TK)
        kd = kid_vm[pl.ds(k0, TK), :].astype(BF16)
        score = jnp.zeros((TQ, TK), F32)
        for h in range(IDX_HEADS):
            s = lax.dot_general(qim_vm[h], kd, (((1,), (1,)), ((), ())), preferred_element_type=F32)
            w = ex_ref[:, EX_WI + h:EX_WI + h + 1]
            score = score + w * jnp.maximum(s, 0.0)
        score = jnp.where(col0 + j * TK <= row, score, -jnp.inf)
        key_vm[j] = _sortable(score)
        return carry

    lax.fori_loop(0, nj, score_chunk, 0)

    RH = 128
    assert seq // LANES <= 256
    ones_b = jnp.ones((LANES, LANES), BF16)

    def count_rows(hit_fn):
        parts = []
        for r0 in range(0, TQ, RH):
            def body(j, acc, r0=r0):
                for cb in range(TK // LANES):
                    blk = key_vm[j, r0:r0 + RH, cb * LANES:(cb + 1) * LANES]
                    acc = acc + jnp.where(hit_fn(blk, j, r0, cb), 1, 0)
                return acc
            parts.append(lax.fori_loop(0, nj, body, jnp.zeros((RH, LANES), jnp.int32)))
        acc = jnp.concatenate(parts, axis=0)
        tot = jnp.dot(acc.astype(F32).astype(BF16), ones_b, preferred_element_type=F32)
        return tot.astype(jnp.int32)

    def count_ge(c):
        return count_rows(lambda blk, j, r0, cb: blk >= c[r0:r0 + RH])

    ans = jnp.where(count_ge(jnp.zeros((TQ, LANES), jnp.int32)) >= topk, 0, INT_MIN).astype(jnp.int32)

    def bit_step(it, ans):
        cand = ans | (jnp.int32(1) << (30 - it))
        return jnp.where(count_ge(cand) >= topk, cand, ans)

    thr_d = lax.fori_loop(0, 31, bit_step, ans)
    thr = thr_d[:, 0:1]

    cut_vm[...] = jnp.full((TQ, 1), seq, jnp.int32)
    n_ge = count_ge(thr_d)

    @pl.when(jnp.max(n_ge) > topk)
    def _():
        need = topk - count_ge(thr_d + 1)
        colh = lax.broadcasted_iota(jnp.int32, (RH, LANES), 1)

        def count_eq_below(x):
            return count_rows(lambda blk, j, r0, cb: (blk == thr_d[r0:r0 + RH])
                              & (colh + (j * TK + cb * LANES) < x[r0:r0 + RH]))

        nbits = max(1, (seq - 1).bit_length())

        def cut_step(it, x):
            cand = x | (jnp.int32(1) << (nbits - 1 - it))
            return jnp.where(count_eq_below(cand) < need, cand, x)

        cut_d = lax.fori_loop(0, nbits, cut_step, jnp.zeros((TQ, LANES), jnp.int32))
        cut_vm[...] = cut_d[:, 0:1]

    cut = cut_vm[...]

    m_vm[...] = jnp.full(m_vm.shape, M_INIT, F32)
    l_vm[...] = jnp.zeros(l_vm.shape, F32)
    acc_vm[...] = jnp.zeros(acc_vm.shape, F32)

    def attend(j, carry):
        k0 = pl.multiple_of(j * TK, TK)
        col = col0 + j * TK
        key = key_vm[j]
        sel = ((key > thr) | ((key == thr) & (col <= cut))) & (col <= row)
        mask_vm[...] = jnp.where(sel, 0.0, MASKED)
        for h in range(DSA_HEADS):
            p = h // 2
            k2 = k_vm[pl.ds(k0, TK), p * 128:(p + 1) * 128]
            v2 = v_vm[pl.ds(k0, TK), p * 128:(p + 1) * 128]
            tiles = []
            for ra in range(TQ // 128):
                rowt = []
                for cb in range(TK // 128):
                    delta = jnp.clip(i * (TQ // 128) + ra - j * (TK // 128) - cb, 0, N_DELTA - 1)
                    rowt.append(tz_vm[delta, h])
                tiles.append(jnp.concatenate(rowt, axis=1))
            bias = jnp.concatenate(tiles, axis=0)
            s = _qk(qm_vm[h], k2) + bias + mask_vm[...]
            m, l, acc = _flash_step(s, v2, m_vm[h], l_vm[h], acc_vm[h])
            m_vm[h] = m
            l_vm[h] = l
            acc_vm[h] = acc
        return carry

    lax.fori_loop(0, nj, attend, 0)

    for p in range(4):
        o0 = acc_vm[2 * p] / l_vm[2 * p]
        o1 = acc_vm[2 * p + 1] / l_vm[2 * p + 1]
        o_ref[:, p * 128:(p + 1) * 128] = jnp.where(_half_mask((TQ, 128), 0), o0, o1).astype(o_ref.dtype)


def _dsa(main, extra, toeplitz, batch, seq):
    TQ, TK = ATT_TQ, ATT_TK
    nq = seq // TQ
    nk = seq // TK
    n = batch * seq
    topk = min(DSA_TOPK_MAX, seq // 4)
    return pl.pallas_call(
        functools.partial(_dsa_kernel, seq=seq, topk=topk),
        grid=(batch, nq),
        in_specs=[pl.BlockSpec((TQ, 512), lambda b, i: (b * nq + i, AB_QB // 512)),
                  pl.BlockSpec((TQ, 512), lambda b, i: (b * nq + i, AB_QI // 512)),
                  pl.BlockSpec((TQ, 128), lambda b, i: (b * nq + i, 1)),
                  pl.BlockSpec(memory_space=pl.ANY),
                  pl.BlockSpec(memory_space=pl.ANY),
                  pl.BlockSpec(memory_space=pl.ANY)],
        out_specs=pl.BlockSpec((TQ, 512), lambda b, i: (b * nq + i, 0)),
        out_shape=jax.ShapeDtypeStruct((n, 512), BF16),
        scratch_shapes=[pltpu.VMEM((seq, 512), BF16), pltpu.VMEM((seq, 512), BF16),
                        pltpu.VMEM((seq, 128), F32),
                        pltpu.VMEM((N_DELTA, DSA_HEADS, 128, 128), F32),
                        pltpu.VMEM((nk, TQ, TK), jnp.int32),
                        pltpu.VMEM((DSA_HEADS, TQ, 128), BF16), pltpu.VMEM((IDX_HEADS, TQ, 128), BF16),
                        pltpu.VMEM((TQ, TK), F32),
                        pltpu.VMEM((DSA_HEADS, TQ, 1), F32), pltpu.VMEM((DSA_HEADS, TQ, 1), F32),
                        pltpu.VMEM((DSA_HEADS, TQ, 128), F32),
                        pltpu.VMEM((TQ, 1), jnp.int32),
                        pltpu.SemaphoreType.DMA((4,))],
        compiler_params=_cparams(("arbitrary", "arbitrary"), VMEM_LIMIT),
        name="dsa_attention",
    )(main, main, extra, main, extra, toeplitz)


DIL_TQ = 128


def _dil_kernel(q_ref, kp_ref, kc_ref, vp_ref, vc_ref, bm_ref, o_ref, ld_ref):
    i = pl.program_id(2)
    TQ = DIL_TQ
    col = lax.broadcasted_iota(jnp.int32, (TQ, 2 * TQ), 1)
    first = jnp.where((i == 0) & (col < TQ), MASKED, 0.0)
    for p in range(DIL_HEADS // 2):
        sl = slice(p * 128, (p + 1) * 128)
        q2 = q_ref[:, sl]
        kc = jnp.concatenate([kp_ref[:, sl], kc_ref[:, sl]], axis=0)
        vc = jnp.concatenate([vp_ref[:, sl], vc_ref[:, sl]], axis=0)
        outs, lds = [], []
        for half in range(2):
            h = 2 * p + half
            qm = jnp.where(_half_mask((TQ, 128), half), q2 * QK_SCALE, jnp.zeros_like(q2))
            lg = _qk(qm, kc) + bm_ref[h] + first
            m = jnp.max(lg, axis=-1, keepdims=True)
            e = jnp.exp(lg - m)
            s = jnp.sum(e, axis=-1, keepdims=True)
            outs.append(jnp.dot(e.astype(BF16), vc, preferred_element_type=F32) / s)
            lds.append(jnp.broadcast_to(m + jnp.log(s), (TQ, 128)))
        lo = _half_mask((TQ, 128), 0)
        o_ref[:, sl] = jnp.where(lo, outs[0], outs[1])
        ld_ref[:, sl] = jnp.where(lo, lds[0], lds[1])


def _dilated_pattern(main, biasmask, dil, batch, seq):
    TQ = DIL_TQ
    n = batch * seq
    ld_seq = seq // dil
    nblk = ld_seq // TQ
    view = main.reshape(n // dil, dil * CD_MAIN)
    cpb = CD_MAIN // 512

    def qmap(b, r, i):
        return (b * nblk + i, r * cpb + CD_QD // 512)

    def kmap(off):
        def f(b, r, i):
            return (b * nblk + jnp.maximum(i - 1 + off, 0), r * cpb + CD_KD // 512)
        return f

    def vmap_(off):
        def f(b, r, i):
            return (b * nblk + jnp.maximum(i - 1 + off, 0), r * cpb + CD_VD // 512)
        return f

    o, ld = pl.pallas_call(
        _dil_kernel,
        grid=(batch, dil, nblk),
        in_specs=[pl.BlockSpec((TQ, 512), qmap),
                  pl.BlockSpec((TQ, 512), kmap(0)), pl.BlockSpec((TQ, 512), kmap(1)),
                  pl.BlockSpec((TQ, 512), vmap_(0)), pl.BlockSpec((TQ, 512), vmap_(1)),
                  pl.BlockSpec((DIL_HEADS, TQ, 2 * TQ), lambda b, r, i: (0, 0, 0))],
        out_specs=[pl.BlockSpec((TQ, 512), lambda b, r, i: (b * nblk + i, r)),
                   pl.BlockSpec((TQ, 512), lambda b, r, i: (b * nblk + i, r))],
        out_shape=[jax.ShapeDtypeStruct((n // dil, dil * 512), F32),
                   jax.ShapeDtypeStruct((n // dil, dil * 512), F32)],
        compiler_params=_cparams(("parallel", "parallel", "parallel"), VMEM_LIMIT),
        name="dilated_attention",
    )(view, view, view, view, view, biasmask)
    return o.reshape(n, 512), ld.reshape(n, 512)


def _dil_merge_kernel(o0, l0, o1, l1, o2, l2, out_ref):
    a, b, c = l0[...], l1[...], l2[...]
    mx = jnp.maximum(jnp.maximum(a, b), c)
    ea, eb, ec = jnp.exp(a - mx), jnp.exp(b - mx), jnp.exp(c - mx)
    tot = ea + eb + ec
    out_ref[...] = ((ea * o0[...] + eb * o1[...] + ec * o2[...]) / tot).astype(out_ref.dtype)


def _dilated(main, bvec, batch, seq):
    n = batch * seq
    parts = []
    t = DIL_TQ
    pad = jnp.full((REL_HEADS, t - 1), MASKED, F32)
    for window, dil in DIL_PATTERNS:
        assert window // dil == t
        steps = bvec[:, BVEC_SHIFT:BVEC_SHIFT + dil * (t + 1):dil]
        w = jnp.flip(jnp.concatenate([pad, steps, pad], axis=1), axis=-1)
        w = jnp.concatenate([w, w[:, :1]], axis=-1)
        skew = jnp.tile(w, (1, t))[:, :t * (3 * t - 1)].reshape(REL_HEADS, t, 3 * t - 1)
        biasmask = skew[:, :, t - 1:3 * t - 1]
        parts.extend(_dilated_pattern(main, biasmask, dil, batch, seq))
    tm = 512
    spec = pl.BlockSpec((tm, 512), lambda i: (i, 0))
    return pl.pallas_call(
        _dil_merge_kernel,
        grid=(n // tm,),
        in_specs=[spec] * 6,
        out_specs=spec,
        out_shape=jax.ShapeDtypeStruct((n, 512), BF16),
        compiler_params=_cparams(("parallel",)),
        name="dilated_merge",
    )(*parts)


MOE_TM = 512
MOE_TF = 896


def _split3(a):
    hi = a.astype(BF16)
    r1 = a - hi.astype(F32)
    mid = r1.astype(BF16)
    lo = (r1 - mid.astype(F32)).astype(BF16)
    return hi, mid, lo


def _router_kernel(x_ref, w_ref, idx_ref, gate_ref):
    xh, xm, xl = _split3(x_ref[...])
    wh, wm, wl = _split3(w_ref[...])
    dot = functools.partial(jnp.dot, preferred_element_type=F32)
    logits = (dot(xh, wh) + (dot(xh, wm) + dot(xm, wh))
              + (dot(xh, wl) + dot(xm, wm) + dot(xl, wh)))
    lane = lax.broadcasted_iota(jnp.int32, logits.shape, 1)
    lg = jnp.where(lane < N_EXPERTS, logits, -jnp.inf)
    v1 = jnp.max(lg, axis=-1, keepdims=True)
    i1 = jnp.min(jnp.where(lg == v1, lane, LANES), axis=-1, keepdims=True)
    lg2 = jnp.where(lane == i1, -jnp.inf, lg)
    v2 = jnp.max(lg2, axis=-1, keepdims=True)
    i2 = jnp.min(jnp.where(lg2 == v2, lane, LANES), axis=-1, keepdims=True)
    e2 = jnp.exp(v2 - v1)
    den = 1.0 + e2
    idx_ref[...] = jnp.where(lane == 0, i1, jnp.where(lane == 1, i2, 0))
    gate_ref[...] = jnp.where(lane == 0, 1.0 / den, jnp.where(lane == 1, e2 / den, 0.0))


def _router(x, w_router_pad, tm=512):
    n, d = x.shape
    return pl.pallas_call(
        _router_kernel,
        grid=(n // tm,),
        in_specs=[pl.BlockSpec((tm, d), lambda i: (i, 0)),
                  pl.BlockSpec((d, LANES), lambda i: (0, 0))],
        out_specs=[pl.BlockSpec((tm, LANES), lambda i: (i, 0)),
                   pl.BlockSpec((tm, LANES), lambda i: (i, 0))],
        out_shape=[jax.ShapeDtypeStruct((n, LANES), jnp.int32),
                   jax.ShapeDtypeStruct((n, LANES), F32)],
        compiler_params=_cparams(("parallel",), VMEM_LIMIT),
        name="moe_router",
    )(x, w_router_pad)


MOE_TC = 512
MOE_WIN = 640


def _moe_ffn_kernel(te_ref, nv_ref, clo_ref, chi_ref, src_ref, x_hbm, w1_ref, w3_ref, w2_ref, gate_ref,
                    hi_ref, lo_ref, acc_ref, xs_ref, xbuf, sems):
    t = pl.program_id(0)
    f = pl.program_id(1)
    last = pl.num_programs(1) - 1
    live = t < nv_ref[0]
    tm = acc_ref.shape[0]

    def fetch(c, slot):
        return pltpu.make_async_copy(x_hbm.at[pl.ds(c * MOE_TC, MOE_TC)], xbuf.at[slot], sems.at[slot])

    @pl.when(live & (f == 0))
    def _():
        clo = clo_ref[t]
        chi = chi_ref[t]
        acc_ref[...] = jnp.zeros_like(acc_ref)
        fetch(clo, 0).start()
        src = src_ref[...]
        lane = lax.broadcasted_iota(jnp.int32, (tm, MOE_TC), 1)

        def body(c, carry):
            slot = (c - clo) & 1

            @pl.when(c < chi)
            def _():
                fetch(c + 1, 1 - slot).start()

            fetch(c, slot).wait()
            onehot = jnp.where(src == lane + c * MOE_TC, 1.0, 0.0).astype(BF16)
            acc_ref[...] += jnp.dot(onehot, xbuf[slot], preferred_element_type=F32)
            return carry

        lax.fori_loop(clo, chi + 1, body, 0)
        xs_ref[...] = acc_ref[...].astype(BF16)

    @pl.when(f == 0)
    def _():
        acc_ref[...] = jnp.zeros_like(acc_ref)

    @pl.when(live)
    def _():
        xb = xs_ref[...]
        a = jnp.dot(xb, w1_ref[...], preferred_element_type=F32)
        c = jnp.dot(xb, w3_ref[...], preferred_element_type=F32)
        h = (_silu(a) * c).astype(BF16)
        acc_ref[...] += jnp.dot(h, w2_ref[...], preferred_element_type=F32)

    @pl.when(f == last)
    def _():
        y = acc_ref[...] * gate_ref[...]
        hi = y.astype(BF16)
        hi_ref[...] = hi
        lo_ref[...] = (y - hi.astype(F32)).astype(BF16)


def _moe_ffn(xb, row_src, w1, w3, w2, gate_sorted, tile_expert, n_live, chunk_lo, chunk_hi):
    p_rows = row_src.shape[0]
    d = xb.shape[1]
    ff = w1.shape[2]
    tm, tf = MOE_TM, MOE_TF

    def rows(t, f, te, nv, clo, chi):
        return (t, 0)

    return pl.pallas_call(
        _moe_ffn_kernel,
        grid_spec=pltpu.PrefetchScalarGridSpec(
            num_scalar_prefetch=4,
            grid=(p_rows // tm, ff // tf),
            in_specs=[pl.BlockSpec((tm, 1), rows),
                      pl.BlockSpec(memory_space=pl.ANY),
                      pl.BlockSpec((None, d, tf), lambda t, f, te, nv, clo, chi: (te[t], 0, f)),
                      pl.BlockSpec((None, d, tf), lambda t, f, te, nv, clo, chi: (te[t], 0, f)),
                      pl.BlockSpec((None, tf, d), lambda t, f, te, nv, clo, chi: (te[t], f, 0)),
                      pl.BlockSpec((tm, 1), rows)],
            out_specs=[pl.BlockSpec((tm, d), rows), pl.BlockSpec((tm, d), rows)],
            scratch_shapes=[pltpu.VMEM((tm, d), F32), pltpu.VMEM((tm, d), BF16),
                            pltpu.VMEM((2, MOE_TC, d), BF16), pltpu.SemaphoreType.DMA((2,))],
        ),
        out_shape=[jax.ShapeDtypeStruct((p_rows, d), BF16), jax.ShapeDtypeStruct((p_rows, d), BF16)],
        compiler_params=_cparams(("arbitrary", "arbitrary"), VMEM_LIMIT),
        name="moe_grouped_swiglu",
    )(tile_expert, n_live, chunk_lo, chunk_hi, row_src, xb, w1, w3, w2, gate_sorted)


def _combine_ln_kernel(win_ref, pos_ref, eid_ref, hi_hbm, lo_hbm, x_ref, g_ref, b_ref, xf_ref, xb_ref,
                       acc_ref, hbuf, lbuf, sems):
    blk = pl.program_id(0)
    tm = acc_ref.shape[0]

    def fetch(e, slot):
        w = pl.multiple_of(win_ref[blk * N_EXPERTS + e], 16)
        return (pltpu.make_async_copy(hi_hbm.at[pl.ds(w, MOE_WIN)], hbuf.at[slot], sems.at[0, slot]),
                pltpu.make_async_copy(lo_hbm.at[pl.ds(w, MOE_WIN)], lbuf.at[slot], sems.at[1, slot]))

    for cp in fetch(0, 0):
        cp.start()
    pos0 = pos_ref[:, 0:1]
    pos1 = pos_ref[:, 1:2]
    lane = lax.broadcasted_iota(jnp.int32, (tm, MOE_WIN), 1)
    acc_ref[...] = ALPHA * x_ref[...]
    for e in range(N_EXPERTS):
        slot = e & 1
        if e + 1 < N_EXPERTS:
            for cp in fetch(e + 1, 1 - slot):
                cp.start()
        for cp in fetch(e, slot):
            cp.wait()
        w = win_ref[blk * N_EXPERTS + e]
        rel0 = jnp.where(eid_ref[:, 0:1] == e, pos0 - w, -1)
        rel1 = jnp.where(eid_ref[:, 1:2] == e, pos1 - w, -1)
        onehot = jnp.where((lane == rel0) | (lane == rel1), 1.0, 0.0).astype(BF16)
        acc_ref[...] += (jnp.dot(onehot, hbuf[slot], preferred_element_type=F32)
                         + jnp.dot(onehot, lbuf[slot], preferred_element_type=F32))
    y = _layernorm_rows(acc_ref[...], g_ref[...], b_ref[...])
    xf_ref[...] = y
    xb_ref[...] = y.astype(BF16)


def _combine_ln(win, pos2, eid2, ys_hi, ys_lo, x, g, b):
    n, d = x.shape
    tm = MOE_TC
    return pl.pallas_call(
        _combine_ln_kernel,
        grid_spec=pltpu.PrefetchScalarGridSpec(
            num_scalar_prefetch=1,
            grid=(n // tm,),
            in_specs=[pl.BlockSpec((tm, TOP_K), lambda i, w: (i, 0)),
                      pl.BlockSpec((tm, TOP_K), lambda i, w: (i, 0)),
                      pl.BlockSpec(memory_space=pl.ANY),
                      pl.BlockSpec(memory_space=pl.ANY),
                      pl.BlockSpec((tm, d), lambda i, w: (i, 0)),
                      pl.BlockSpec((1, d), lambda i, w: (0, 0)),
                      pl.BlockSpec((1, d), lambda i, w: (0, 0))],
            out_specs=[pl.BlockSpec((tm, d), lambda i, w: (i, 0)),
                       pl.BlockSpec((tm, d), lambda i, w: (i, 0))],
            scratch_shapes=[pltpu.VMEM((tm, d), F32),
                            pltpu.VMEM((2, MOE_WIN, d), BF16), pltpu.VMEM((2, MOE_WIN, d), BF16),
                            pltpu.SemaphoreType.DMA((2, 2))],
        ),
        out_shape=[jax.ShapeDtypeStruct((n, d), F32), jax.ShapeDtypeStruct((n, d), BF16)],
        compiler_params=_cparams(("arbitrary",), VMEM_LIMIT),
        name="moe_combine_ln",
    )(win, pos2, eid2, ys_hi, ys_lo, x, g, b)


def _moe(x, xb, w_router_pad, w1, w3, w2, g, b):
    n, d = x.shape
    tm = MOE_TM
    idx, gates = _router(x, w_router_pad)
    e_flat = idx[:, :TOP_K].reshape(-1)
    g_flat = gates[:, :TOP_K].reshape(-1)
    onehot = (e_flat[:, None] == jnp.arange(N_EXPERTS, dtype=jnp.int32)[None, :]).astype(jnp.int32)
    csum = jnp.cumsum(onehot, axis=0)
    rank = jnp.take_along_axis(csum, e_flat[:, None], axis=1)[:, 0] - 1
    counts = csum[-1]
    padded = ((counts + tm - 1) // tm) * tm
    ends = jnp.cumsum(padded)
    pos = ((ends - padded)[e_flat] + rank).astype(jnp.int32)
    tiles = (TOP_K * n) // tm + N_EXPERTS + 2
    p_rows = tiles * tm
    token = jnp.arange(TOP_K * n, dtype=jnp.int32) // TOP_K
    row_src = jnp.full((p_rows,), -1, jnp.int32).at[pos].set(token)
    gate_sorted = jnp.zeros((p_rows,), F32).at[pos].set(g_flat).reshape(p_rows, 1)
    tile_start = jnp.arange(tiles, dtype=jnp.int32) * tm
    tile_expert = jnp.minimum(jnp.searchsorted(ends, tile_start, side="right"), N_EXPERTS - 1).astype(jnp.int32)
    n_live = (ends[-1] // tm).astype(jnp.int32).reshape(1)
    per_tile = row_src.reshape(tiles, tm)
    chunk_lo = (jnp.min(jnp.where(per_tile >= 0, per_tile, n - 1), axis=1) // MOE_TC).astype(jnp.int32)
    chunk_hi = (jnp.max(per_tile, axis=1) // MOE_TC).astype(jnp.int32)
    big = jnp.int32(p_rows)
    wkey = (token // MOE_TC) * N_EXPERTS + e_flat
    win = jnp.full(((n // MOE_TC) * N_EXPERTS,), big, jnp.int32).at[wkey].min(pos)
    win = (jnp.where(win == big, 0, win) // 16) * 16

    ys_hi, ys_lo = _moe_ffn(xb, row_src.reshape(p_rows, 1), w1, w3, w2, gate_sorted, tile_expert, n_live,
                            chunk_lo, chunk_hi)
    return _combine_ln(win, pos.reshape(n, TOP_K), idx[:, :TOP_K], ys_hi, ys_lo, x, g, b)


def _regroup_ab(w):
    main = jnp.concatenate([w[:, 0:1536], w[:, 1552:3600]], axis=1)
    ki = w[:, 3600:3664]
    extra = jnp.concatenate([ki, ki, w[:, 1536:1552], w[:, 3664:3672],
                             jnp.zeros((w.shape[0], 128 - GLA_GATE_RANK - IDX_HEADS), w.dtype)], axis=1)
    return main.astype(BF16), extra.astype(BF16)


def _regroup_cd(w):
    main = jnp.concatenate([w[:, 0:1536], w[:, 1544:3080]], axis=1)
    extra = jnp.concatenate([w[:, 1536:1544], jnp.zeros((w.shape[0], 128 - FOX_HEADS), w.dtype)], axis=1)
    return main.astype(BF16), extra.astype(BF16)


def _toeplitz_tiles(bvec):
    t = 128
    seg = jnp.stack([bvec[:, dl * t:dl * t + 2 * t - 1] for dl in range(N_DELTA - 1)], axis=1)
    w = jnp.flip(seg, axis=-1)
    w = jnp.concatenate([w, w[..., :1]], axis=-1)
    skew = jnp.tile(w, (1, 1, t))[..., :t * (2 * t - 1)].reshape(REL_HEADS, N_DELTA - 1, t, 2 * t - 1)
    near = skew[..., t - 1:]
    far = jnp.broadcast_to(bvec[:, BVEC_LEN - 1][:, None, None, None], (REL_HEADS, 1, t, t))
    return jnp.concatenate([near, far], axis=1).transpose(1, 0, 2, 3)


def kernel(x, ln_g, ln_b, rel_table, w_in_ab, w_gate_a, b_gate_a, g_norm_a, w_out_ab, w_in_cd, b_forget,
           w_out_cd, w1_dense, w3_dense, w2_dense, w_router, w1_moe, w3_moe, w2_moe):
    batch, seq, d = x.shape
    n = batch * seq
    xf = x.reshape(n, d)
    xb = xf.astype(BF16)

    bvec = _bias_vector(rel_table)
    toeplitz = _toeplitz_tiles(bvec)

    for layer in range(DEPTH):
        j = layer // 2
        g0, b0 = ln_g[layer, 0][None, :], ln_b[layer, 0][None, :]
        g1, b1 = ln_g[layer, 1][None, :], ln_b[layer, 1][None, :]
        if layer % 2 == 0:
            w_main, w_extra = _regroup_ab(w_in_ab[j])
            main = _matmul(xb, w_main, BF16, 512, w_main.shape[1])
            extra = _matmul(xb, w_extra, F32, 512, AB_EXTRA)
            oa = _gla(main, extra, w_gate_a[j], b_gate_a[j][None, :], g_norm_a[j][None, :], batch, seq)
            ob = _dsa(main, extra, toeplitz, batch, seq)
            o = jnp.concatenate([oa, ob], axis=1)
            xf, xb = _outproj_ln(o, w_out_ab[j].astype(BF16), xf, g0, b0)
            xf, xb = _swiglu_ln(xb, w1_dense[j].astype(BF16), w3_dense[j].astype(BF16),
                                w2_dense[j].astype(BF16), xf, g1, b1)
        else:
            w_main, w_extra = _regroup_cd(w_in_cd[j])
            main = _matmul(xb, w_main, BF16, 512, w_main.shape[1])
            extra = _matmul(xb, w_extra, F32, 512, CD_EXTRA)
            bf_row = jnp.concatenate([b_forget[j], jnp.zeros((128 - FOX_HEADS,), F32)])[None, :]
            fcum = _forget_cumsum(extra, bf_row, batch, seq)
            oc = _fox(main, fcum, batch, seq)
            od = _dilated(main, bvec, batch, seq)
            o = jnp.concatenate([oc, od], axis=1)
            xf, xb = _outproj_ln(o, w_out_cd[j].astype(BF16), xf, g0, b0)
            w_r = jnp.concatenate([w_router[j], jnp.zeros((d, LANES - N_EXPERTS), F32)], axis=1)
            xf, xb = _moe(xf, xb, w_r, w1_moe[j].astype(BF16), w3_moe[j].astype(BF16),
                          w2_moe[j].astype(BF16), g1, b1)
    return xf.reshape(batch, seq, d)
```

```python
import functools
import math

import jax
import jax.numpy as jnp
from jax import lax
from jax.experimental import pallas as pl
from jax.experimental.pallas import tpu as pltpu

D_MODEL = 1024
DEPTH = 4
HEAD_DIM = 64
GLA_HEADS = 4
GLA_DK = 64
GLA_DV = 128
GLA_GATE_RANK = 16
GLA_GATE_TAU = 16.0
GLA_CHUNK = 64
DSA_HEADS = 8
IDX_HEADS = 8
IDX_DIM = 64
DSA_TOPK_MAX = 256
FOX_HEADS = 8
DIL_HEADS = 8
DIL_PATTERNS = ((128, 1), (512, 4), (2048, 16))
REL_BUCKETS = 32
REL_MAX_DIST = 2048
REL_HEADS = 8
D_FF = 2816
N_EXPERTS = 8
TOP_K = 2
D_FF_EXPERT = 3584
ALPHA = (2 * DEPTH) ** 0.25
LN_EPS = 1e-5

LANES = 128
MASKED = -1e30
M_INIT = -1e20
VMEM_LIMIT = 56 * 1024 * 1024

F32 = jnp.float32
BF16 = jnp.bfloat16

AB_QA, AB_KA, AB_VA, AB_RA, AB_QB, AB_KB, AB_VB, AB_QI = 0, 256, 512, 1024, 1536, 2048, 2560, 3072
AB_MAIN = 3584
AB_EXTRA = 256
EX_GA = 0
EX_WI = 16
CD_QC, CD_KC, CD_VC, CD_QD, CD_KD, CD_VD = 0, 512, 1024, 1536, 2048, 2560
CD_MAIN = 3072
CD_EXTRA = 128


def _cparams(sem, vmem=None):
    return pltpu.CompilerParams(dimension_semantics=sem, vmem_limit_bytes=vmem)


def _mm_kernel(x_ref, w_ref, o_ref):
    o_ref[...] = jnp.dot(x_ref[...], w_ref[...], preferred_element_type=F32).astype(o_ref.dtype)


def _matmul(x, w, out_dtype, tm, tn):
    n, k = x.shape
    m = w.shape[1]
    return pl.pallas_call(
        _mm_kernel,
        grid=(n // tm, m // tn),
        in_specs=[pl.BlockSpec((tm, k), lambda i, j: (i, 0)),
                  pl.BlockSpec((k, tn), lambda i, j: (0, j))],
        out_specs=pl.BlockSpec((tm, tn), lambda i, j: (i, j)),
        out_shape=jax.ShapeDtypeStruct((n, m), out_dtype),
        compiler_params=_cparams(("parallel", "parallel"), VMEM_LIMIT),
        name="proj_matmul",
    )(x, w)


def _layernorm_rows(z, g, b):
    mu = jnp.mean(z, axis=-1, keepdims=True)
    zc = z - mu
    var = jnp.mean(zc * zc, axis=-1, keepdims=True)
    return zc * lax.rsqrt(var + LN_EPS) * g + b


def _outproj_ln_kernel(o_ref, w_ref, x_ref, g_ref, b_ref, xf_ref, xb_ref):
    mix = jnp.dot(o_ref[...], w_ref[...], preferred_element_type=F32)
    y = _layernorm_rows(ALPHA * x_ref[...] + mix, g_ref[...], b_ref[...])
    xf_ref[...] = y
    xb_ref[...] = y.astype(BF16)


def _outproj_ln(o, w, x, g, b, tm=512):
    n, k = o.shape
    d = w.shape[1]
    return pl.pallas_call(
        _outproj_ln_kernel,
        grid=(n // tm,),
        in_specs=[pl.BlockSpec((tm, k), lambda i: (i, 0)),
                  pl.BlockSpec((k, d), lambda i: (0, 0)),
                  pl.BlockSpec((tm, d), lambda i: (i, 0)),
                  pl.BlockSpec((1, d), lambda i: (0, 0)),
                  pl.BlockSpec((1, d), lambda i: (0, 0))],
        out_specs=[pl.BlockSpec((tm, d), lambda i: (i, 0)),
                   pl.BlockSpec((tm, d), lambda i: (i, 0))],
        out_shape=[jax.ShapeDtypeStruct((n, d), F32), jax.ShapeDtypeStruct((n, d), BF16)],
        compiler_params=_cparams(("parallel",), VMEM_LIMIT),
        name="outproj_ln",
    )(o, w, x, g, b)


def _silu(a):
    return a / (1.0 + jnp.exp(-a))


def _swiglu_ln_kernel(xb_ref, w1_ref, w3_ref, w2_ref, x_ref, g_ref, b_ref, xf_ref, xbo_ref, acc_ref):
    f = pl.program_id(1)

    @pl.when(f == 0)
    def _():
        acc_ref[...] = jnp.zeros_like(acc_ref)

    xb = xb_ref[...]
    a = jnp.dot(xb, w1_ref[...], preferred_element_type=F32)
    c = jnp.dot(xb, w3_ref[...], preferred_element_type=F32)
    h = (_silu(a) * c).astype(BF16)
    acc_ref[...] += jnp.dot(h, w2_ref[...], preferred_element_type=F32)

    @pl.when(f == pl.num_programs(1) - 1)
    def _():
        y = _layernorm_rows(ALPHA * x_ref[...] + acc_ref[...], g_ref[...], b_ref[...])
        xf_ref[...] = y
        xbo_ref[...] = y.astype(BF16)


def _swiglu_ln(xb, w1, w3, w2, x, g, b, tm=512, tf=1408):
    n, d = xb.shape
    ff = w1.shape[1]
    return pl.pallas_call(
        _swiglu_ln_kernel,
        grid=(n // tm, ff // tf),
        in_specs=[pl.BlockSpec((tm, d), lambda i, f: (i, 0)),
                  pl.BlockSpec((d, tf), lambda i, f: (0, f)),
                  pl.BlockSpec((d, tf), lambda i, f: (0, f)),
                  pl.BlockSpec((tf, d), lambda i, f: (f, 0)),
                  pl.BlockSpec((tm, d), lambda i, f: (i, 0)),
                  pl.BlockSpec((1, d), lambda i, f: (0, 0)),
                  pl.BlockSpec((1, d), lambda i, f: (0, 0))],
        out_specs=[pl.BlockSpec((tm, d), lambda i, f: (i, 0)),
                   pl.BlockSpec((tm, d), lambda i, f: (i, 0))],
        out_shape=[jax.ShapeDtypeStruct((n, d), F32), jax.ShapeDtypeStruct((n, d), BF16)],
        scratch_shapes=[pltpu.VMEM((tm, d), F32)],
        compiler_params=_cparams(("parallel", "arbitrary"), VMEM_LIMIT),
        name="swiglu_ln",
    )(xb, w1, w3, w2, x, g, b)


BVEC_LEN = 2304
BVEC_SHIFT = 127


def _bvec_kernel(tab_ref, o_ref):
    n = lax.broadcasted_iota(jnp.int32, (REL_HEADS, BVEC_LEN), 1)
    d = jnp.maximum(n - BVEC_SHIFT, 0)
    max_exact = REL_BUCKETS // 2
    df = jnp.maximum(d, 1).astype(F32)
    large = max_exact + (jnp.log(df / max_exact) / math.log(REL_MAX_DIST / max_exact)
                         * (REL_BUCKETS - max_exact)).astype(jnp.int32)
    large = jnp.minimum(large, REL_BUCKETS - 1)
    bucket = jnp.where(d < max_exact, d, large)
    acc = jnp.zeros((REL_HEADS, BVEC_LEN), F32)
    for k in range(REL_BUCKETS):
        acc = jnp.where(bucket == k, tab_ref[:, k:k + 1], acc)
    o_ref[...] = acc


def _bias_vector(rel_table):
    return pl.pallas_call(
        _bvec_kernel,
        out_shape=jax.ShapeDtypeStruct((REL_HEADS, BVEC_LEN), F32),
        name="t5_bias_vector",
    )(rel_table.T)


GLA_TB = 256


def _log_sigmoid(z):
    return jnp.minimum(z, 0.0) - jnp.log(1.0 + jnp.exp(-jnp.abs(z)))


def _cumsum_rows(x):
    rows = x.shape[0]
    row = lax.broadcasted_iota(jnp.int32, x.shape, 0)
    sh = 1
    while sh < rows:
        x = x + jnp.where(row >= sh, pltpu.roll(x, sh, 0), 0.0)
        sh *= 2
    return x


def _gla_kernel(q_ref, k_ref, v_ref, r_ref, ga_ref, wg_ref, bg_ref, gn_ref, o_ref, s_ref, a_ref):
    c_id = pl.program_id(2)

    @pl.when(c_id == 0)
    def _():
        s_ref[...] = jnp.zeros_like(s_ref)

    C = GLA_CHUNK
    srow = lax.broadcasted_iota(jnp.int32, (2 * GLA_DK, 2 * GLA_DV), 0)
    scol = lax.broadcasted_iota(jnp.int32, (2 * GLA_DK, 2 * GLA_DV), 1)
    blockdiag = (srow < GLA_DK) == (scol < GLA_DV)
    trow = lax.broadcasted_iota(jnp.int32, (C, C), 0)
    tcol = lax.broadcasted_iota(jnp.int32, (C, C), 1)
    causal = tcol <= trow

    def chunk(c, carry):
        r0 = pl.multiple_of(c * C, C)
        q2 = q_ref[pl.ds(r0, C), :].astype(F32) * (GLA_DK ** -0.5)
        k2 = k_ref[pl.ds(r0, C), :].astype(F32)
        v2 = v_ref[pl.ds(r0, C), :]
        ga = ga_ref[pl.ds(r0, C), :][:, EX_GA:EX_GA + GLA_GATE_RANK]
        z = jnp.dot(ga.astype(BF16), wg_ref[...].astype(BF16), preferred_element_type=F32) + bg_ref[...]
        g = _log_sigmoid(z) / GLA_GATE_TAU
        G = _cumsum_rows(g)
        GT = G.T
        kT = k2.T
        qT = q2.T
        for t in range(C):
            dec = jnp.exp(GT[:, t:t + 1] - GT)
            prod = (qT[:, t:t + 1] * kT) * dec
            a_ref[0, t:t + 1, :] = jnp.sum(prod[:GLA_DK], axis=0, keepdims=True)
            a_ref[1, t:t + 1, :] = jnp.sum(prod[GLA_DK:], axis=0, keepdims=True)
        S = s_ref[...]
        o_inter = jnp.dot((q2 * jnp.exp(G)).astype(BF16), S.astype(BF16), preferred_element_type=F32)
        o_intra = []
        for h in range(2):
            A = jnp.where(causal, a_ref[h], 0.0)
            o_intra.append(jnp.dot(A.astype(BF16), v2[:, h * GLA_DV:(h + 1) * GLA_DV],
                                   preferred_element_type=F32))
        o2 = o_inter + jnp.concatenate(o_intra, axis=1)
        g_last = GT[:, C - 1:C]
        kd = kT * jnp.exp(g_last - GT)
        upd = jnp.dot(kd.astype(BF16), v2, preferred_element_type=F32)
        s_ref[...] = jnp.exp(g_last) * S + jnp.where(blockdiag, upd, 0.0)
        outs = []
        for h in range(2):
            of = o2[:, h * GLA_DV:(h + 1) * GLA_DV]
            of = of * lax.rsqrt(jnp.mean(of * of, axis=-1, keepdims=True) + LN_EPS) * gn_ref[...]
            rr = r_ref[pl.ds(r0, C), h * GLA_DV:(h + 1) * GLA_DV].astype(F32)
            outs.append(of * _silu(rr))
        o_ref[pl.ds(r0, C), :] = jnp.concatenate(outs, axis=1).astype(o_ref.dtype)
        return carry

    lax.fori_loop(0, GLA_TB // C, chunk, 0)


def _gla(main, extra, w_gate, b_gate, g_norm, batch, seq):
    nblk = seq // GLA_TB
    n = batch * seq

    def rows(b, hp, c):
        return b * nblk + c

    return pl.pallas_call(
        _gla_kernel,
        grid=(batch, 2, nblk),
        in_specs=[
            pl.BlockSpec((GLA_TB, 128), lambda b, hp, c: (rows(b, hp, c), AB_QA // 128 + hp)),
            pl.BlockSpec((GLA_TB, 128), lambda b, hp, c: (rows(b, hp, c), AB_KA // 128 + hp)),
            pl.BlockSpec((GLA_TB, 256), lambda b, hp, c: (rows(b, hp, c), AB_VA // 256 + hp)),
            pl.BlockSpec((GLA_TB, 256), lambda b, hp, c: (rows(b, hp, c), AB_RA // 256 + hp)),
            pl.BlockSpec((GLA_TB, 128), lambda b, hp, c: (rows(b, hp, c), 1)),
            pl.BlockSpec((GLA_GATE_RANK, 128), lambda b, hp, c: (0, hp)),
            pl.BlockSpec((1, 128), lambda b, hp, c: (0, hp)),
            pl.BlockSpec((1, GLA_DV), lambda b, hp, c: (0, 0)),
        ],
        out_specs=pl.BlockSpec((GLA_TB, 256), lambda b, hp, c: (rows(b, hp, c), hp)),
        out_shape=jax.ShapeDtypeStruct((n, GLA_HEADS * GLA_DV), BF16),
        scratch_shapes=[pltpu.VMEM((2 * GLA_DK, 2 * GLA_DV), F32),
                        pltpu.VMEM((2, GLA_CHUNK, GLA_CHUNK), F32)],
        compiler_params=_cparams(("parallel", "parallel", "arbitrary"), VMEM_LIMIT),
        name="gla",
    )(main, main, main, main, extra, w_gate, b_gate, g_norm)


ATT_TQ = 256
ATT_TK = 1024
FOX_TQ = 512
FOX_TK = 1024


def _half_mask(shape, half):
    lane = lax.broadcasted_iota(jnp.int32, shape, len(shape) - 1)
    return (lane < HEAD_DIM) if half == 0 else (lane >= HEAD_DIM)


def _flash_step(s, v2, m, l, acc):
    m_new = jnp.maximum(m, jnp.max(s, axis=-1, keepdims=True))
    alpha = jnp.exp(m - m_new)
    p = jnp.exp(s - m_new)
    l = alpha * l + jnp.sum(p, axis=-1, keepdims=True)
    acc = alpha * acc + jnp.dot(p.astype(BF16), v2, preferred_element_type=F32)
    return m_new, l, acc


QK_SCALE = HEAD_DIM ** -0.5


def _qk(qm, k2):
    return lax.dot_general(qm, k2, (((1,), (1,)), ((), ())), preferred_element_type=F32)


def _load_kv(b, seq, srcs, dsts, sems):
    copies = []
    for n, (src, col, width) in enumerate(srcs):
        cp = pltpu.make_async_copy(src.at[pl.ds(b * seq, seq), pl.ds(col, width)], dsts[n], sems.at[n])
        cp.start()
        copies.append(cp)
    for cp in copies:
        cp.wait()


def _fcum_kernel(ex_ref, bf_ref, o_ref, carry_ref):
    i = pl.program_id(1)

    @pl.when(i == 0)
    def _():
        carry_ref[...] = jnp.zeros_like(carry_ref)

    lf = _log_sigmoid(ex_ref[...] + bf_ref[...])
    F = _cumsum_rows(lf) + carry_ref[...]
    o_ref[...] = F
    carry_ref[...] = F[F.shape[0] - 1:, :]


def _forget_cumsum(extra, b_forget_row, batch, seq, tb=256):
    nblk = seq // tb
    return pl.pallas_call(
        _fcum_kernel,
        grid=(batch, nblk),
        in_specs=[pl.BlockSpec((tb, 128), lambda b, i: (b * nblk + i, 0)),
                  pl.BlockSpec((1, 128), lambda b, i: (0, 0))],
        out_specs=pl.BlockSpec((tb, 128), lambda b, i: (b * nblk + i, 0)),
        out_shape=jax.ShapeDtypeStruct((batch * seq, 128), F32),
        scratch_shapes=[pltpu.VMEM((1, 128), F32)],
        compiler_params=_cparams(("parallel", "arbitrary")),
        name="forget_cumsum",
    )(extra, b_forget_row)


def _fox_kernel(q_ref, fq_ref, fk_ref, kv_hbm, o_ref, k_vm, v_vm, sems, *, seq):
    b = pl.program_id(0)
    i = pl.program_id(1)
    TQ, TK = FOX_TQ, FOX_TK

    @pl.when(i == 0)
    def _():
        _load_kv(b, seq, [(kv_hbm, CD_KC, 512), (kv_hbm, CD_VC, 512)], [k_vm, v_vm], sems)

    nj = (i * TQ + TQ - 1) // TK + 1
    row = i * TQ + lax.broadcasted_iota(jnp.int32, (TQ, TK), 0)
    col0 = lax.broadcasted_iota(jnp.int32, (TQ, TK), 1)
    for p in range(FOX_HEADS // 2):
        q2 = q_ref[:, p * 128:(p + 1) * 128]
        qms = [jnp.where(_half_mask((TQ, 128), half), q2 * QK_SCALE, jnp.zeros_like(q2)) for half in range(2)]
        fqs = [fq_ref[:, 2 * p + half:2 * p + half + 1] for half in range(2)]

        def body(j, carry, diagonal=False, qms=qms, fqs=fqs, p=p):
            k0 = pl.multiple_of(j * TK, TK)
            k2 = k_vm[pl.ds(k0, TK), p * 128:(p + 1) * 128]
            v2 = v_vm[pl.ds(k0, TK), p * 128:(p + 1) * 128]
            out = []
            for half in range(2):
                m, l, acc = carry[half]
                s = _qk(qms[half], k2) + (fqs[half] - fk_ref[2 * p + half, pl.ds(j, 1), :])
                if diagonal:
                    s = jnp.where(col0 + j * TK <= row, s, MASKED)
                out.append(_flash_step(s, v2, m, l, acc))
            return tuple(out)

        init = (jnp.full((TQ, 1), M_INIT, F32), jnp.zeros((TQ, 1), F32), jnp.zeros((TQ, 128), F32))
        carry = lax.fori_loop(0, nj - 1, body, (init, init))
        (_, l0, acc0), (_, l1, acc1) = body(nj - 1, carry, diagonal=True)
        o_ref[:, p * 128:(p + 1) * 128] = jnp.where(_half_mask((TQ, 128), 0), acc0 / l0, acc1 / l1).astype(o_ref.dtype)


def _fox(main, fcum, batch, seq):
    TQ, TK = FOX_TQ, FOX_TK
    nq = seq // TQ
    nk = seq // TK
    n = batch * seq
    fk = fcum[:, :FOX_HEADS].reshape(batch, seq, FOX_HEADS).transpose(0, 2, 1).reshape(batch * FOX_HEADS, nk, TK)
    return pl.pallas_call(
        functools.partial(_fox_kernel, seq=seq),
        grid=(batch, nq),
        in_specs=[pl.BlockSpec((TQ, 512), lambda b, i: (b * nq + i, CD_QC // 512)),
                  pl.BlockSpec((TQ, 128), lambda b, i: (b * nq + i, 0)),
                  pl.BlockSpec((FOX_HEADS, nk, TK), lambda b, i: (b, 0, 0)),
                  pl.BlockSpec(memory_space=pl.ANY)],
        out_specs=pl.BlockSpec((TQ, 512), lambda b, i: (b * nq + i, 0)),
        out_shape=jax.ShapeDtypeStruct((n, 512), BF16),
        scratch_shapes=[pltpu.VMEM((seq, 512), BF16), pltpu.VMEM((seq, 512), BF16),
                        pltpu.SemaphoreType.DMA((2,))],
        compiler_params=_cparams(("arbitrary", "arbitrary"), VMEM_LIMIT),
        name="fox_attention",
    )(main, fcum, fk, main)


INT_MIN = -(2 ** 31)
N_DELTA = 18


def _sortable(x):
    b = pltpu.bitcast(x, jnp.int32)
    return b ^ ((b >> 31) & 0x7FFFFFFF)


def _fold_lanes(x):
    out = x[:, 0:LANES]
    for c in range(1, x.shape[1] // LANES):
        out = out + x[:, c * LANES:(c + 1) * LANES]
    return out


def _dsa_kernel(q_ref, qi_ref, ex_ref, kv_hbm, kid_hbm, tz_hbm, o_ref,
                k_vm, v_vm, kid_vm, tz_vm, key_vm, qm_vm, qim_vm, mask_vm, m_vm, l_vm, acc_vm, cut_vm, sems,
                *, seq, topk):
    b = pl.program_id(0)
    i = pl.program_id(1)
    TQ, TK = ATT_TQ, ATT_TK

    @pl.when(i == 0)
    def _():
        _load_kv(b, seq, [(kv_hbm, AB_KB, 512), (kv_hbm, AB_VB, 512), (kid_hbm, 0, 128)],
                 [k_vm, v_vm, kid_vm], sems)

    @pl.when((i == 0) & (b == 0))
    def _():
        cp = pltpu.make_async_copy(tz_hbm, tz_vm, sems.at[3])
        cp.start()
        cp.wait()

    nj = (i * TQ + TQ - 1) // TK + 1
    row = i * TQ + lax.broadcasted_iota(jnp.int32, (TQ, TK), 0)
    col0 = lax.broadcasted_iota(jnp.int32, (TQ, TK), 1)

    for p in range(4):
        q2 = q_ref[:, p * 128:(p + 1) * 128]
        qi2 = qi_ref[:, p * 128:(p + 1) * 128]
        for half in range(2):
            keep = _half_mask((TQ, 128), half)
            qm_vm[2 * p + half] = jnp.where(keep, q2 * QK_SCALE, jnp.zeros_like(q2))
            qim_vm[2 * p + half] = jnp.where(keep, qi2, jnp.zeros_like(qi2))

    def score_chunk(j, carry):
        k0 = pl.multiple_of(j * TK, TK)
        kd = kid_vm[pl.ds(k0, TK), :].astype(BF16)
        score = jnp.zeros((TQ, TK), F32)
        for h in range(IDX_HEADS):
            s = lax.dot_general(qim_vm[h], kd, (((1,), (1,)), ((), ())), preferred_element_type=F32)
            w = ex_ref[:, EX_WI + h:EX_WI + h + 1]
            score = score + w * jnp.maximum(s, 0.0)
        score = jnp.where(col0 + j * TK <= row, score, -jnp.inf)
        key_vm[j] = _sortable(score)
        return carry

    lax.fori_loop(0, nj, score_chunk, 0)

    RH = 128
    assert seq // LANES <= 256
    ones_b = jnp.ones((LANES, LANES), BF16)

    def count_rows(hit_fn):
        parts = []
        for r0 in range(0, TQ, RH):
            def body(j, acc, r0=r0):
                for cb in range(TK // LANES):
                    blk = key_vm[j, r0:r0 + RH, cb * LANES:(cb + 1) * LANES]
                    acc = acc + jnp.where(hit_fn(blk, j, r0, cb), 1, 0)
                return acc
            parts.append(lax.fori_loop(0, nj, body, jnp.zeros((RH, LANES), jnp.int32)))
        acc = jnp.concatenate(parts, axis=0)
        tot = jnp.dot(acc.astype(F32).astype(BF16), ones_b, preferred_element_type=F32)
        return tot.astype(jnp.int32)

    def count_ge(c):
        return count_rows(lambda blk, j, r0, cb: blk >= c[r0:r0 + RH])

    ans = jnp.where(count_ge(jnp.zeros((TQ, LANES), jnp.int32)) >= topk, 0, INT_MIN).astype(jnp.int32)

    def bit_step(it, ans):
        cand = ans | (jnp.int32(1) << (30 - it))
        return jnp.where(count_ge(cand) >= topk, cand, ans)

    thr_d = lax.fori_loop(0, 31, bit_step, ans)
    thr = thr_d[:, 0:1]

    cut_vm[...] = jnp.full((TQ, 1), seq, jnp.int32)
    n_ge = count_ge(thr_d)

    @pl.when(jnp.max(n_ge) > topk)
    def _():
        need = topk - count_ge(thr_d + 1)
        colh = lax.broadcasted_iota(jnp.int32, (RH, LANES), 1)

        def count_eq_below(x):
            return count_rows(lambda blk, j, r0, cb: (blk == thr_d[r0:r0 + RH])
                              & (colh + (j * TK + cb * LANES) < x[r0:r0 + RH]))

        nbits = max(1, (seq - 1).bit_length())

        def cut_step(it, x):
            cand = x | (jnp.int32(1) << (nbits - 1 - it))
            return jnp.where(count_eq_below(cand) < need, cand, x)

        cut_d = lax.fori_loop(0, nbits, cut_step, jnp.zeros((TQ, LANES), jnp.int32))
        cut_vm[...] = cut_d[:, 0:1]

    cut = cut_vm[...]

    m_vm[...] = jnp.full(m_vm.shape, M_INIT, F32)
    l_vm[...] = jnp.zeros(l_vm.shape, F32)
    acc_vm[...] = jnp.zeros(acc_vm.shape, F32)

    def attend(j, carry):
        k0 = pl.multiple_of(j * TK, TK)
        col = col0 + j * TK
        key = key_vm[j]
        sel = ((key > thr) | ((key == thr) & (col <= cut))) & (col <= row)
        mask_vm[...] = jnp.where(sel, 0.0, MASKED)
        for h in range(DSA_HEADS):
            p = h // 2
            k2 = k_vm[pl.ds(k0, TK), p * 128:(p + 1) * 128]
            v2 = v_vm[pl.ds(k0, TK), p * 128:(p + 1) * 128]
            tiles = []
            for ra in range(TQ // 128):
                rowt = []
                for cb in range(TK // 128):
                    delta = jnp.clip(i * (TQ // 128) + ra - j * (TK // 128) - cb, 0, N_DELTA - 1)
                    rowt.append(tz_vm[delta, h])
                tiles.append(jnp.concatenate(rowt, axis=1))
            bias = jnp.concatenate(tiles, axis=0)
            s = _qk(qm_vm[h], k2) + bias + mask_vm[...]
            m, l, acc = _flash_step(s, v2, m_vm[h], l_vm[h], acc_vm[h])
            m_vm[h] = m
            l_vm[h] = l
            acc_vm[h] = acc
        return carry

    lax.fori_loop(0, nj, attend, 0)

    for p in range(4):
        o0 = acc_vm[2 * p] / l_vm[2 * p]
        o1 = acc_vm[2 * p + 1] / l_vm[2 * p + 1]
        o_ref[:, p * 128:(p + 1) * 128] = jnp.where(_half_mask((TQ, 128), 0), o0, o1).astype(o_ref.dtype)


def _dsa(main, extra, toeplitz, batch, seq):
    TQ, TK = ATT_TQ, ATT_TK
    nq = seq // TQ
    nk = seq // TK
    n = batch * seq
    topk = min(DSA_TOPK_MAX, seq // 4)
    return pl.pallas_call(
        functools.partial(_dsa_kernel, seq=seq, topk=topk),
        grid=(batch, nq),
        in_specs=[pl.BlockSpec((TQ, 512), lambda b, i: (b * nq + i, AB_QB // 512)),
                  pl.BlockSpec((TQ, 512), lambda b, i: (b * nq + i, AB_QI // 512)),
                  pl.BlockSpec((TQ, 128), lambda b, i: (b * nq + i, 1)),
                  pl.BlockSpec(memory_space=pl.ANY),
                  pl.BlockSpec(memory_space=pl.ANY),
                  pl.BlockSpec(memory_space=pl.ANY)],
        out_specs=pl.BlockSpec((TQ, 512), lambda b, i: (b * nq + i, 0)),
        out_shape=jax.ShapeDtypeStruct((n, 512), BF16),
        scratch_shapes=[pltpu.VMEM((seq, 512), BF16), pltpu.VMEM((seq, 512), BF16),
                        pltpu.VMEM((seq, 128), F32),
                        pltpu.VMEM((N_DELTA, DSA_HEADS, 128, 128), F32),
                        pltpu.VMEM((nk, TQ, TK), jnp.int32),
                        pltpu.VMEM((DSA_HEADS, TQ, 128), BF16), pltpu.VMEM((IDX_HEADS, TQ, 128), BF16),
                        pltpu.VMEM((TQ, TK), F32),
                        pltpu.VMEM((DSA_HEADS, TQ, 1), F32), pltpu.VMEM((DSA_HEADS, TQ, 1), F32),
                        pltpu.VMEM((DSA_HEADS, TQ, 128), F32),
                        pltpu.VMEM((TQ, 1), jnp.int32),
                        pltpu.SemaphoreType.DMA((4,))],
        compiler_params=_cparams(("arbitrary", "arbitrary"), VMEM_LIMIT),
        name="dsa_attention",
    )(main, main, extra, main, extra, toeplitz)


DIL_TQ = 128


def _dil_kernel(q_ref, kp_ref, kc_ref, vp_ref, vc_ref, bm_ref, o_ref, ld_ref):
    i = pl.program_id(2)
    TQ = DIL_TQ
    col = lax.broadcasted_iota(jnp.int32, (TQ, 2 * TQ), 1)
    first = jnp.where((i == 0) & (col < TQ), MASKED, 0.0)
    for p in range(DIL_HEADS // 2):
        sl = slice(p * 128, (p + 1) * 128)
        q2 = q_ref[:, sl]
        kc = jnp.concatenate([kp_ref[:, sl], kc_ref[:, sl]], axis=0)
        vc = jnp.concatenate([vp_ref[:, sl], vc_ref[:, sl]], axis=0)
        outs, lds = [], []
        for half in range(2):
            h = 2 * p + half
            qm = jnp.where(_half_mask((TQ, 128), half), q2 * QK_SCALE, jnp.zeros_like(q2))
            lg = _qk(qm, kc) + bm_ref[h] + first
            m = jnp.max(lg, axis=-1, keepdims=True)
            e = jnp.exp(lg - m)
            s = jnp.sum(e, axis=-1, keepdims=True)
            outs.append(jnp.dot(e.astype(BF16), vc, preferred_element_type=F32) / s)
            lds.append(jnp.broadcast_to(m + jnp.log(s), (TQ, 128)))
        lo = _half_mask((TQ, 128), 0)
        o_ref[:, sl] = jnp.where(lo, outs[0], outs[1])
        ld_ref[:, sl] = jnp.where(lo, lds[0], lds[1])


def _dilated_pattern(main, biasmask, dil, batch, seq):
    TQ = DIL_TQ
    n = batch * seq
    ld_seq = seq // dil
    nblk = ld_seq // TQ
    view = main.reshape(n // dil, dil * CD_MAIN)
    cpb = CD_MAIN // 512

    def qmap(b, r, i):
        return (b * nblk + i, r * cpb + CD_QD // 512)

    def kmap(off):
        def f(b, r, i):
            return (b * nblk + jnp.maximum(i - 1 + off, 0), r * cpb + CD_KD // 512)
        return f

    def vmap_(off):
        def f(b, r, i):
            return (b * nblk + jnp.maximum(i - 1 + off, 0), r * cpb + CD_VD // 512)
        return f

    o, ld = pl.pallas_call(
        _dil_kernel,
        grid=(batch, dil, nblk),
        in_specs=[pl.BlockSpec((TQ, 512), qmap),
                  pl.BlockSpec((TQ, 512), kmap(0)), pl.BlockSpec((TQ, 512), kmap(1)),
                  pl.BlockSpec((TQ, 512), vmap_(0)), pl.BlockSpec((TQ, 512), vmap_(1)),
                  pl.BlockSpec((DIL_HEADS, TQ, 2 * TQ), lambda b, r, i: (0, 0, 0))],
        out_specs=[pl.BlockSpec((TQ, 512), lambda b, r, i: (b * nblk + i, r)),
                   pl.BlockSpec((TQ, 512), lambda b, r, i: (b * nblk + i, r))],
        out_shape=[jax.ShapeDtypeStruct((n // dil, dil * 512), F32),
                   jax.ShapeDtypeStruct((n // dil, dil * 512), F32)],
        compiler_params=_cparams(("parallel", "parallel", "parallel"), VMEM_LIMIT),
        name="dilated_attention",
    )(view, view, view, view, view, biasmask)
    return o.reshape(n, 512), ld.reshape(n, 512)


def _dil_merge_kernel(o0, l0, o1, l1, o2, l2, out_ref):
    a, b, c = l0[...], l1[...], l2[...]
    mx = jnp.maximum(jnp.maximum(a, b), c)
    ea, eb, ec = jnp.exp(a - mx), jnp.exp(b - mx), jnp.exp(c - mx)
    tot = ea + eb + ec
    out_ref[...] = ((ea * o0[...] + eb * o1[...] + ec * o2[...]) / tot).astype(out_ref.dtype)


def _dilated(main, bvec, batch, seq):
    n = batch * seq
    parts = []
    t = DIL_TQ
    pad = jnp.full((REL_HEADS, t - 1), MASKED, F32)
    for window, dil in DIL_PATTERNS:
        assert window // dil == t
        steps = bvec[:, BVEC_SHIFT:BVEC_SHIFT + dil * (t + 1):dil]
        w = jnp.flip(jnp.concatenate([pad, steps, pad], axis=1), axis=-1)
        w = jnp.concatenate([w, w[:, :1]], axis=-1)
        skew = jnp.tile(w, (1, t))[:, :t * (3 * t - 1)].reshape(REL_HEADS, t, 3 * t - 1)
        biasmask = skew[:, :, t - 1:3 * t - 1]
        parts.extend(_dilated_pattern(main, biasmask, dil, batch, seq))
    tm = 512
    spec = pl.BlockSpec((tm, 512), lambda i: (i, 0))
    return pl.pallas_call(
        _dil_merge_kernel,
        grid=(n // tm,),
        in_specs=[spec] * 6,
        out_specs=spec,
        out_shape=jax.ShapeDtypeStruct((n, 512), BF16),
        compiler_params=_cparams(("parallel",)),
        name="dilated_merge",
    )(*parts)


MOE_TM = 512
MOE_TF = 512


def _split3(a):
    hi = a.astype(BF16)
    r1 = a - hi.astype(F32)
    mid = r1.astype(BF16)
    lo = (r1 - mid.astype(F32)).astype(BF16)
    return hi, mid, lo


def _router_kernel(x_ref, w_ref, idx_ref, gate_ref):
    xh, xm, xl = _split3(x_ref[...])
    wh, wm, wl = _split3(w_ref[...])
    dot = functools.partial(jnp.dot, preferred_element_type=F32)
    logits = (dot(xh, wh) + (dot(xh, wm) + dot(xm, wh))
              + (dot(xh, wl) + dot(xm, wm) + dot(xl, wh)))
    lane = lax.broadcasted_iota(jnp.int32, logits.shape, 1)
    lg = jnp.where(lane < N_EXPERTS, logits, -jnp.inf)
    v1 = jnp.max(lg, axis=-1, keepdims=True)
    i1 = jnp.min(jnp.where(lg == v1, lane, LANES), axis=-1, keepdims=True)
    lg2 = jnp.where(lane == i1, -jnp.inf, lg)
    v2 = jnp.max(lg2, axis=-1, keepdims=True)
    i2 = jnp.min(jnp.where(lg2 == v2, lane, LANES), axis=-1, keepdims=True)
    e2 = jnp.exp(v2 - v1)
    den = 1.0 + e2
    idx_ref[...] = jnp.where(lane == 0, i1, jnp.where(lane == 1, i2, 0))
    gate_ref[...] = jnp.where(lane == 0, 1.0 / den, jnp.where(lane == 1, e2 / den, 0.0))


def _router(x, w_router_pad, tm=512):
    n, d = x.shape
    return pl.pallas_call(
        _router_kernel,
        grid=(n // tm,),
        in_specs=[pl.BlockSpec((tm, d), lambda i: (i, 0)),
                  pl.BlockSpec((d, LANES), lambda i: (0, 0))],
        out_specs=[pl.BlockSpec((tm, LANES), lambda i: (i, 0)),
                   pl.BlockSpec((tm, LANES), lambda i: (i, 0))],
        out_shape=[jax.ShapeDtypeStruct((n, LANES), jnp.int32),
                   jax.ShapeDtypeStruct((n, LANES), F32)],
        compiler_params=_cparams(("parallel",), VMEM_LIMIT),
        name="moe_router",
    )(x, w_router_pad)


MOE_TC = 512
MOE_WIN = 640


def _moe_ffn_kernel(te_ref, nv_ref, clo_ref, chi_ref, src_ref, x_hbm, w1_ref, w3_ref, w2_ref, gate_ref,
                    hi_ref, lo_ref, acc_ref, xs_ref, xbuf, sems):
    t = pl.program_id(0)
    f = pl.program_id(1)
    last = pl.num_programs(1) - 1
    live = t < nv_ref[0]
    tm = acc_ref.shape[0]

    def fetch(c, slot):
        return pltpu.make_async_copy(x_hbm.at[pl.ds(c * MOE_TC, MOE_TC)], xbuf.at[slot], sems.at[slot])

    @pl.when(live & (f == 0))
    def _():
        clo = clo_ref[t]
        chi = chi_ref[t]
        acc_ref[...] = jnp.zeros_like(acc_ref)
        fetch(clo, 0).start()
        src = src_ref[...]
        lane = lax.broadcasted_iota(jnp.int32, (tm, MOE_TC), 1)

        def body(c, carry):
            slot = (c - clo) & 1

            @pl.when(c < chi)
            def _():
                fetch(c + 1, 1 - slot).start()

            fetch(c, slot).wait()
            onehot = jnp.where(src == lane + c * MOE_TC, 1.0, 0.0).astype(BF16)
            acc_ref[...] += jnp.dot(onehot, xbuf[slot], preferred_element_type=F32)
            return carry

        lax.fori_loop(clo, chi + 1, body, 0)
        xs_ref[...] = acc_ref[...].astype(BF16)

    @pl.when(f == 0)
    def _():
        acc_ref[...] = jnp.zeros_like(acc_ref)

    @pl.when(live)
    def _():
        xb = xs_ref[...]
        a = jnp.dot(xb, w1_ref[...], preferred_element_type=F32)
        c = jnp.dot(xb, w3_ref[...], preferred_element_type=F32)
        h = (_silu(a) * c).astype(BF16)
        acc_ref[...] += jnp.dot(h, w2_ref[...], preferred_element_type=F32)

    @pl.when(f == last)
    def _():
        y = acc_ref[...] * gate_ref[...]
        hi = y.astype(BF16)
        hi_ref[...] = hi
        lo_ref[...] = (y - hi.astype(F32)).astype(BF16)


def _moe_ffn(xb, row_src, w1, w3, w2, gate_sorted, tile_expert, n_live, chunk_lo, chunk_hi):
    p_rows = row_src.shape[0]
    d = xb.shape[1]
    ff = w1.shape[2]
    tm, tf = MOE_TM, MOE_TF

    def rows(t, f, te, nv, clo, chi):
        return (t, 0)

    def wf(t, f, nv):
        return jnp.where(t < nv[0], f, ff // tf - 1)

    return pl.pallas_call(
        _moe_ffn_kernel,
        grid_spec=pltpu.PrefetchScalarGridSpec(
            num_scalar_prefetch=4,
            grid=(p_rows // tm, ff // tf),
            in_specs=[pl.BlockSpec((tm, 1), rows),
                      pl.BlockSpec(memory_space=pl.ANY),
                      pl.BlockSpec((None, d, tf), lambda t, f, te, nv, clo, chi: (te[t], 0, wf(t, f, nv))),
                      pl.BlockSpec((None, d, tf), lambda t, f, te, nv, clo, chi: (te[t], 0, wf(t, f, nv))),
                      pl.BlockSpec((None, tf, d), lambda t, f, te, nv, clo, chi: (te[t], wf(t, f, nv), 0)),
                      pl.BlockSpec((tm, 1), rows)],
            out_specs=[pl.BlockSpec((tm, d), rows), pl.BlockSpec((tm, d), rows)],
            scratch_shapes=[pltpu.VMEM((tm, d), F32), pltpu.VMEM((tm, d), BF16),
                            pltpu.VMEM((2, MOE_TC, d), BF16), pltpu.SemaphoreType.DMA((2,))],
        ),
        out_shape=[jax.ShapeDtypeStruct((p_rows, d), BF16), jax.ShapeDtypeStruct((p_rows, d), BF16)],
        compiler_params=_cparams(("arbitrary", "arbitrary"), VMEM_LIMIT),
        name="moe_grouped_swiglu",
    )(tile_expert, n_live, chunk_lo, chunk_hi, row_src, xb, w1, w3, w2, gate_sorted)


def _combine_ln_kernel(win_ref, pos_ref, eid_ref, hi_hbm, lo_hbm, x_ref, g_ref, b_ref, xf_ref, xb_ref,
                       acc_ref, hbuf, lbuf, sems):
    blk = pl.program_id(0)
    tm = acc_ref.shape[0]

    def fetch(e, slot):
        w = pl.multiple_of(win_ref[blk * N_EXPERTS + e], 16)
        return (pltpu.make_async_copy(hi_hbm.at[pl.ds(w, MOE_WIN)], hbuf.at[slot], sems.at[0, slot]),
                pltpu.make_async_copy(lo_hbm.at[pl.ds(w, MOE_WIN)], lbuf.at[slot], sems.at[1, slot]))

    for cp in fetch(0, 0):
        cp.start()
    pos0 = pos_ref[:, 0:1]
    pos1 = pos_ref[:, 1:2]
    lane = lax.broadcasted_iota(jnp.int32, (tm, MOE_WIN), 1)
    acc_ref[...] = ALPHA * x_ref[...]
    for e in range(N_EXPERTS):
        slot = e & 1
        if e + 1 < N_EXPERTS:
            for cp in fetch(e + 1, 1 - slot):
                cp.start()
        for cp in fetch(e, slot):
            cp.wait()
        w = win_ref[blk * N_EXPERTS + e]
        rel0 = jnp.where(eid_ref[:, 0:1] == e, pos0 - w, -1)
        rel1 = jnp.where(eid_ref[:, 1:2] == e, pos1 - w, -1)
        onehot = jnp.where((lane == rel0) | (lane == rel1), 1.0, 0.0).astype(BF16)
        acc_ref[...] += (jnp.dot(onehot, hbuf[slot], preferred_element_type=F32)
                         + jnp.dot(onehot, lbuf[slot], preferred_element_type=F32))
    y = _layernorm_rows(acc_ref[...], g_ref[...], b_ref[...])
    xf_ref[...] = y
    xb_ref[...] = y.astype(BF16)


def _combine_ln(win, pos2, eid2, ys_hi, ys_lo, x, g, b):
    n, d = x.shape
    tm = MOE_TC
    return pl.pallas_call(
        _combine_ln_kernel,
        grid_spec=pltpu.PrefetchScalarGridSpec(
            num_scalar_prefetch=1,
            grid=(n // tm,),
            in_specs=[pl.BlockSpec((tm, TOP_K), lambda i, w: (i, 0)),
                      pl.BlockSpec((tm, TOP_K), lambda i, w: (i, 0)),
                      pl.BlockSpec(memory_space=pl.ANY),
                      pl.BlockSpec(memory_space=pl.ANY),
                      pl.BlockSpec((tm, d), lambda i, w: (i, 0)),
                      pl.BlockSpec((1, d), lambda i, w: (0, 0)),
                      pl.BlockSpec((1, d), lambda i, w: (0, 0))],
            out_specs=[pl.BlockSpec((tm, d), lambda i, w: (i, 0)),
                       pl.BlockSpec((tm, d), lambda i, w: (i, 0))],
            scratch_shapes=[pltpu.VMEM((tm, d), F32),
                            pltpu.VMEM((2, MOE_WIN, d), BF16), pltpu.VMEM((2, MOE_WIN, d), BF16),
                            pltpu.SemaphoreType.DMA((2, 2))],
        ),
        out_shape=[jax.ShapeDtypeStruct((n, d), F32), jax.ShapeDtypeStruct((n, d), BF16)],
        compiler_params=_cparams(("arbitrary",), VMEM_LIMIT),
        name="moe_combine_ln",
    )(win, pos2, eid2, ys_hi, ys_lo, x, g, b)


def _moe(x, xb, w_router_pad, w1, w3, w2, g, b):
    n, d = x.shape
    tm = MOE_TM
    idx, gates = _router(x, w_router_pad)
    e_flat = idx[:, :TOP_K].reshape(-1)
    g_flat = gates[:, :TOP_K].reshape(-1)
    onehot = (e_flat[:, None] == jnp.arange(N_EXPERTS, dtype=jnp.int32)[None, :]).astype(jnp.int32)
    csum = jnp.cumsum(onehot, axis=0)
    rank = jnp.take_along_axis(csum, e_flat[:, None], axis=1)[:, 0] - 1
    counts = csum[-1]
    padded = ((counts + tm - 1) // tm) * tm
    ends = jnp.cumsum(padded)
    pos = ((ends - padded)[e_flat] + rank).astype(jnp.int32)
    tiles = (TOP_K * n) // tm + N_EXPERTS + 2
    p_rows = tiles * tm
    token = jnp.arange(TOP_K * n, dtype=jnp.int32) // TOP_K
    packed = jnp.stack([token, lax.bitcast_convert_type(g_flat, jnp.int32)], axis=1)
    empty = jnp.broadcast_to(jnp.array([-1, 0], jnp.int32), (p_rows, 2))
    placed = empty.at[pos].set(packed)
    row_src = placed[:, 0]
    gate_sorted = lax.bitcast_convert_type(placed[:, 1], F32).reshape(p_rows, 1)
    tile_start = jnp.arange(tiles, dtype=jnp.int32) * tm
    tile_expert = jnp.minimum(jnp.searchsorted(ends, tile_start, side="right"), N_EXPERTS - 1).astype(jnp.int32)
    n_live = (ends[-1] // tm).astype(jnp.int32).reshape(1)
    tile_expert = jnp.where(tile_start < ends[-1], tile_expert, tile_expert[n_live[0] - 1])
    per_tile = row_src.reshape(tiles, tm)
    chunk_lo = (jnp.min(jnp.where(per_tile >= 0, per_tile, n - 1), axis=1) // MOE_TC).astype(jnp.int32)
    chunk_hi = (jnp.max(per_tile, axis=1) // MOE_TC).astype(jnp.int32)
    big = jnp.int32(p_rows)
    wkey = (token // MOE_TC) * N_EXPERTS + e_flat
    win = jnp.full(((n // MOE_TC) * N_EXPERTS,), big, jnp.int32).at[wkey].min(pos)
    win = (jnp.where(win == big, 0, win) // 16) * 16

    ys_hi, ys_lo = _moe_ffn(xb, row_src.reshape(p_rows, 1), w1, w3, w2, gate_sorted, tile_expert, n_live,
                            chunk_lo, chunk_hi)
    return _combine_ln(win, pos.reshape(n, TOP_K), idx[:, :TOP_K], ys_hi, ys_lo, x, g, b)


def _regroup_ab(w):
    main = jnp.concatenate([w[:, 0:1536], w[:, 1552:3600]], axis=1)
    ki = w[:, 3600:3664]
    extra = jnp.concatenate([ki, ki, w[:, 1536:1552], w[:, 3664:3672],
                             jnp.zeros((w.shape[0], 128 - GLA_GATE_RANK - IDX_HEADS), w.dtype)], axis=1)
    return main.astype(BF16), extra.astype(BF16)


def _regroup_cd(w):
    main = jnp.concatenate([w[:, 0:1536], w[:, 1544:3080]], axis=1)
    extra = jnp.concatenate([w[:, 1536:1544], jnp.zeros((w.shape[0], 128 - FOX_HEADS), w.dtype)], axis=1)
    return main.astype(BF16), extra.astype(BF16)


def _toeplitz_tiles(bvec):
    t = 128
    seg = jnp.stack([bvec[:, dl * t:dl * t + 2 * t - 1] for dl in range(N_DELTA - 1)], axis=1)
    w = jnp.flip(seg, axis=-1)
    w = jnp.concatenate([w, w[..., :1]], axis=-1)
    skew = jnp.tile(w, (1, 1, t))[..., :t * (2 * t - 1)].reshape(REL_HEADS, N_DELTA - 1, t, 2 * t - 1)
    near = skew[..., t - 1:]
    far = jnp.broadcast_to(bvec[:, BVEC_LEN - 1][:, None, None, None], (REL_HEADS, 1, t, t))
    return jnp.concatenate([near, far], axis=1).transpose(1, 0, 2, 3)


def kernel(x, ln_g, ln_b, rel_table, w_in_ab, w_gate_a, b_gate_a, g_norm_a, w_out_ab, w_in_cd, b_forget,
           w_out_cd, w1_dense, w3_dense, w2_dense, w_router, w1_moe, w3_moe, w2_moe):
    batch, seq, d = x.shape
    n = batch * seq
    xf = x.reshape(n, d)
    xb = xf.astype(BF16)

    bvec = _bias_vector(rel_table)
    toeplitz = _toeplitz_tiles(bvec)

    for layer in range(DEPTH):
        j = layer // 2
        g0, b0 = ln_g[layer, 0][None, :], ln_b[layer, 0][None, :]
        g1, b1 = ln_g[layer, 1][None, :], ln_b[layer, 1][None, :]
        if layer % 2 == 0:
            w_main, w_extra = _regroup_ab(w_in_ab[j])
            main = _matmul(xb, w_main, BF16, 512, w_main.shape[1])
            extra = _matmul(xb, w_extra, F32, 512, AB_EXTRA)
            oa = _gla(main, extra, w_gate_a[j], b_gate_a[j][None, :], g_norm_a[j][None, :], batch, seq)
            ob = _dsa(main, extra, toeplitz, batch, seq)
            o = jnp.concatenate([oa, ob], axis=1)
            xf, xb = _outproj_ln(o, w_out_ab[j].astype(BF16), xf, g0, b0)
            xf, xb = _swiglu_ln(xb, w1_dense[j].astype(BF16), w3_dense[j].astype(BF16),
                                w2_dense[j].astype(BF16), xf, g1, b1)
        else:
            w_main, w_extra = _regroup_cd(w_in_cd[j])
            main = _matmul(xb, w_main, BF16, 512, w_main.shape[1])
            extra = _matmul(xb, w_extra, F32, 512, CD_EXTRA)
            bf_row = jnp.concatenate([b_forget[j], jnp.zeros((128 - FOX_HEADS,), F32)])[None, :]
            fcum = _forget_cumsum(extra, bf_row, batch, seq)
            oc = _fox(main, fcum, batch, seq)
            od = _dilated(main, bvec, batch, seq)
            o = jnp.concatenate([oc, od], axis=1)
            xf, xb = _outproj_ln(o, w_out_cd[j].astype(BF16), xf, g0, b0)
            w_r = jnp.concatenate([w_router[j], jnp.zeros((d, LANES - N_EXPERTS), F32)], axis=1)
            xf, xb = _moe(xf, xb, w_r, w1_moe[j].astype(BF16), w3_moe[j].astype(BF16),
                          w2_moe[j].astype(BF16), g1, b1)
    return xf.reshape(batch, seq, d)
```

```python
import functools
import math

import jax
import jax.numpy as jnp
from jax import lax
from jax.experimental import pallas as pl
from jax.experimental.pallas import tpu as pltpu

D_MODEL = 1024
DEPTH = 4
HEAD_DIM = 64
GLA_HEADS = 4
GLA_DK = 64
GLA_DV = 128
GLA_GATE_RANK = 16
GLA_GATE_TAU = 16.0
GLA_CHUNK = 64
DSA_HEADS = 8
IDX_HEADS = 8
IDX_DIM = 64
DSA_TOPK_MAX = 256
FOX_HEADS = 8
DIL_HEADS = 8
DIL_PATTERNS = ((128, 1), (512, 4), (2048, 16))
REL_BUCKETS = 32
REL_MAX_DIST = 2048
REL_HEADS = 8
D_FF = 2816
N_EXPERTS = 8
TOP_K = 2
D_FF_EXPERT = 3584
ALPHA = (2 * DEPTH) ** 0.25
LN_EPS = 1e-5

LANES = 128
MASKED = -1e30
M_INIT = -1e20
VMEM_LIMIT = 56 * 1024 * 1024

F32 = jnp.float32
BF16 = jnp.bfloat16

AB_QA, AB_KA, AB_VA, AB_RA, AB_QB, AB_KB, AB_VB, AB_QI = 0, 256, 512, 1024, 1536, 2048, 2560, 3072
AB_MAIN = 3584
AB_EXTRA = 256
EX_GA = 0
EX_WI = 16
CD_QC, CD_KC, CD_VC, CD_QD, CD_KD, CD_VD = 0, 512, 1024, 1536, 2048, 2560
CD_MAIN = 3072
CD_EXTRA = 128


def _cparams(sem, vmem=None):
    return pltpu.CompilerParams(dimension_semantics=sem, vmem_limit_bytes=vmem)


def _mm_kernel(x_ref, w_ref, o_ref):
    o_ref[...] = jnp.dot(x_ref[...], w_ref[...], preferred_element_type=F32).astype(o_ref.dtype)


def _matmul(x, w, out_dtype, tm, tn):
    n, k = x.shape
    m = w.shape[1]
    return pl.pallas_call(
        _mm_kernel,
        grid=(n // tm, m // tn),
        in_specs=[pl.BlockSpec((tm, k), lambda i, j: (i, 0)),
                  pl.BlockSpec((k, tn), lambda i, j: (0, j))],
        out_specs=pl.BlockSpec((tm, tn), lambda i, j: (i, j)),
        out_shape=jax.ShapeDtypeStruct((n, m), out_dtype),
        compiler_params=_cparams(("parallel", "parallel"), VMEM_LIMIT),
        name="proj_matmul",
    )(x, w)


def _layernorm_rows(z, g, b):
    mu = jnp.mean(z, axis=-1, keepdims=True)
    zc = z - mu
    var = jnp.mean(zc * zc, axis=-1, keepdims=True)
    return zc * lax.rsqrt(var + LN_EPS) * g + b


def _outproj_ln_kernel(oa_ref, ob_ref, w_ref, x_ref, g_ref, b_ref, xf_ref, xb_ref):
    ka = oa_ref.shape[1]
    mix = (jnp.dot(oa_ref[...], w_ref[:ka, :], preferred_element_type=F32)
           + jnp.dot(ob_ref[...], w_ref[ka:, :], preferred_element_type=F32))
    y = _layernorm_rows(ALPHA * x_ref[...] + mix, g_ref[...], b_ref[...])
    xf_ref[...] = y
    xb_ref[...] = y.astype(BF16)


def _outproj_ln(oa, ob, w, x, g, b, tm=512):
    n, ka = oa.shape
    kb = ob.shape[1]
    d = w.shape[1]
    return pl.pallas_call(
        _outproj_ln_kernel,
        grid=(n // tm,),
        in_specs=[pl.BlockSpec((tm, ka), lambda i: (i, 0)),
                  pl.BlockSpec((tm, kb), lambda i: (i, 0)),
                  pl.BlockSpec((ka + kb, d), lambda i: (0, 0)),
                  pl.BlockSpec((tm, d), lambda i: (i, 0)),
                  pl.BlockSpec((1, d), lambda i: (0, 0)),
                  pl.BlockSpec((1, d), lambda i: (0, 0))],
        out_specs=[pl.BlockSpec((tm, d), lambda i: (i, 0)),
                   pl.BlockSpec((tm, d), lambda i: (i, 0))],
        out_shape=[jax.ShapeDtypeStruct((n, d), F32), jax.ShapeDtypeStruct((n, d), BF16)],
        compiler_params=_cparams(("parallel",), VMEM_LIMIT),
        name="outproj_ln",
    )(oa, ob, w, x, g, b)


def _silu(a):
    return a / (1.0 + jnp.exp(-a))


def _swiglu_ln_kernel(xb_ref, w1_ref, w3_ref, w2_ref, x_ref, g_ref, b_ref, xf_ref, xbo_ref, acc_ref):
    f = pl.program_id(1)

    @pl.when(f == 0)
    def _():
        acc_ref[...] = jnp.zeros_like(acc_ref)

    xb = xb_ref[...]
    a = jnp.dot(xb, w1_ref[...], preferred_element_type=F32)
    c = jnp.dot(xb, w3_ref[...], preferred_element_type=F32)
    h = (_silu(a) * c).astype(BF16)
    acc_ref[...] += jnp.dot(h, w2_ref[...], preferred_element_type=F32)

    @pl.when(f == pl.num_programs(1) - 1)
    def _():
        y = _layernorm_rows(ALPHA * x_ref[...] + acc_ref[...], g_ref[...], b_ref[...])
        xf_ref[...] = y
        xbo_ref[...] = y.astype(BF16)


def _swiglu_ln(xb, w1, w3, w2, x, g, b, tm=512, tf=1408):
    n, d = xb.shape
    ff = w1.shape[1]
    return pl.pallas_call(
        _swiglu_ln_kernel,
        grid=(n // tm, ff // tf),
        in_specs=[pl.BlockSpec((tm, d), lambda i, f: (i, 0)),
                  pl.BlockSpec((d, tf), lambda i, f: (0, f)),
                  pl.BlockSpec((d, tf), lambda i, f: (0, f)),
                  pl.BlockSpec((tf, d), lambda i, f: (f, 0)),
                  pl.BlockSpec((tm, d), lambda i, f: (i, 0)),
                  pl.BlockSpec((1, d), lambda i, f: (0, 0)),
                  pl.BlockSpec((1, d), lambda i, f: (0, 0))],
        out_specs=[pl.BlockSpec((tm, d), lambda i, f: (i, 0)),
                   pl.BlockSpec((tm, d), lambda i, f: (i, 0))],
        out_shape=[jax.ShapeDtypeStruct((n, d), F32), jax.ShapeDtypeStruct((n, d), BF16)],
        scratch_shapes=[pltpu.VMEM((tm, d), F32)],
        compiler_params=_cparams(("parallel", "arbitrary"), VMEM_LIMIT),
        name="swiglu_ln",
    )(xb, w1, w3, w2, x, g, b)


BVEC_LEN = 2304
BVEC_SHIFT = 127


def _bvec_kernel(tab_ref, o_ref):
    n = lax.broadcasted_iota(jnp.int32, (REL_HEADS, BVEC_LEN), 1)
    d = jnp.maximum(n - BVEC_SHIFT, 0)
    max_exact = REL_BUCKETS // 2
    df = jnp.maximum(d, 1).astype(F32)
    large = max_exact + (jnp.log(df / max_exact) / math.log(REL_MAX_DIST / max_exact)
                         * (REL_BUCKETS - max_exact)).astype(jnp.int32)
    large = jnp.minimum(large, REL_BUCKETS - 1)
    bucket = jnp.where(d < max_exact, d, large)
    acc = jnp.zeros((REL_HEADS, BVEC_LEN), F32)
    for k in range(REL_BUCKETS):
        acc = jnp.where(bucket == k, tab_ref[:, k:k + 1], acc)
    o_ref[...] = acc


def _bias_vector(rel_table):
    return pl.pallas_call(
        _bvec_kernel,
        out_shape=jax.ShapeDtypeStruct((REL_HEADS, BVEC_LEN), F32),
        name="t5_bias_vector",
    )(rel_table.T)


GLA_TB = 256


def _log_sigmoid(z):
    return jnp.minimum(z, 0.0) - jnp.log(1.0 + jnp.exp(-jnp.abs(z)))


def _cumsum_rows(x):
    rows = x.shape[0]
    row = lax.broadcasted_iota(jnp.int32, x.shape, 0)
    sh = 1
    while sh < rows:
        x = x + jnp.where(row >= sh, pltpu.roll(x, sh, 0), 0.0)
        sh *= 2
    return x


def _gla_kernel(q_ref, k_ref, v_ref, r_ref, ga_ref, wg_ref, bg_ref, gn_ref, o_ref, s_ref, a_ref):
    c_id = pl.program_id(2)

    @pl.when(c_id == 0)
    def _():
        s_ref[...] = jnp.zeros_like(s_ref)

    C = GLA_CHUNK
    srow = lax.broadcasted_iota(jnp.int32, (2 * GLA_DK, 2 * GLA_DV), 0)
    scol = lax.broadcasted_iota(jnp.int32, (2 * GLA_DK, 2 * GLA_DV), 1)
    blockdiag = (srow < GLA_DK) == (scol < GLA_DV)
    trow = lax.broadcasted_iota(jnp.int32, (C, C), 0)
    tcol = lax.broadcasted_iota(jnp.int32, (C, C), 1)
    causal = tcol <= trow

    def chunk(c, carry):
        r0 = pl.multiple_of(c * C, C)
        q2 = q_ref[pl.ds(r0, C), :].astype(F32) * (GLA_DK ** -0.5)
        k2 = k_ref[pl.ds(r0, C), :].astype(F32)
        v2 = v_ref[pl.ds(r0, C), :]
        ga = ga_ref[pl.ds(r0, C), :][:, EX_GA:EX_GA + GLA_GATE_RANK]
        z = jnp.dot(ga.astype(BF16), wg_ref[...].astype(BF16), preferred_element_type=F32) + bg_ref[...]
        g = _log_sigmoid(z) / GLA_GATE_TAU
        G = _cumsum_rows(g)
        GT = G.T
        kT = k2.T
        qT = q2.T
        for t in range(C):
            dec = jnp.exp(GT[:, t:t + 1] - GT)
            prod = (qT[:, t:t + 1] * kT) * dec
            a_ref[0, t:t + 1, :] = jnp.sum(prod[:GLA_DK], axis=0, keepdims=True)
            a_ref[1, t:t + 1, :] = jnp.sum(prod[GLA_DK:], axis=0, keepdims=True)
        S = s_ref[...]
        o_inter = jnp.dot((q2 * jnp.exp(G)).astype(BF16), S.astype(BF16), preferred_element_type=F32)
        o_intra = []
        for h in range(2):
            A = jnp.where(causal, a_ref[h], 0.0)
            o_intra.append(jnp.dot(A.astype(BF16), v2[:, h * GLA_DV:(h + 1) * GLA_DV],
                                   preferred_element_type=F32))
        o2 = o_inter + jnp.concatenate(o_intra, axis=1)
        g_last = GT[:, C - 1:C]
        kd = kT * jnp.exp(g_last - GT)
        upd = jnp.dot(kd.astype(BF16), v2, preferred_element_type=F32)
        s_ref[...] = jnp.exp(g_last) * S + jnp.where(blockdiag, upd, 0.0)
        outs = []
        for h in range(2):
            of = o2[:, h * GLA_DV:(h + 1) * GLA_DV]
            of = of * lax.rsqrt(jnp.mean(of * of, axis=-1, keepdims=True) + LN_EPS) * gn_ref[...]
            rr = r_ref[pl.ds(r0, C), h * GLA_DV:(h + 1) * GLA_DV].astype(F32)
            outs.append(of * _silu(rr))
        o_ref[pl.ds(r0, C), :] = jnp.concatenate(outs, axis=1).astype(o_ref.dtype)
        return carry

    lax.fori_loop(0, GLA_TB // C, chunk, 0)


def _gla(main, extra, w_gate, b_gate, g_norm, batch, seq):
    nblk = seq // GLA_TB
    n = batch * seq

    def rows(b, hp, c):
        return b * nblk + c

    return pl.pallas_call(
        _gla_kernel,
        grid=(batch, 2, nblk),
        in_specs=[
            pl.BlockSpec((GLA_TB, 128), lambda b, hp, c: (rows(b, hp, c), AB_QA // 128 + hp)),
            pl.BlockSpec((GLA_TB, 128), lambda b, hp, c: (rows(b, hp, c), AB_KA // 128 + hp)),
            pl.BlockSpec((GLA_TB, 256), lambda b, hp, c: (rows(b, hp, c), AB_VA // 256 + hp)),
            pl.BlockSpec((GLA_TB, 256), lambda b, hp, c: (rows(b, hp, c), AB_RA // 256 + hp)),
            pl.BlockSpec((GLA_TB, 128), lambda b, hp, c: (rows(b, hp, c), 1)),
            pl.BlockSpec((GLA_GATE_RANK, 128), lambda b, hp, c: (0, hp)),
            pl.BlockSpec((1, 128), lambda b, hp, c: (0, hp)),
            pl.BlockSpec((1, GLA_DV), lambda b, hp, c: (0, 0)),
        ],
        out_specs=pl.BlockSpec((GLA_TB, 256), lambda b, hp, c: (rows(b, hp, c), hp)),
        out_shape=jax.ShapeDtypeStruct((n, GLA_HEADS * GLA_DV), BF16),
        scratch_shapes=[pltpu.VMEM((2 * GLA_DK, 2 * GLA_DV), F32),
                        pltpu.VMEM((2, GLA_CHUNK, GLA_CHUNK), F32)],
        compiler_params=_cparams(("parallel", "parallel", "arbitrary"), VMEM_LIMIT),
        name="gla",
    )(main, main, main, main, extra, w_gate, b_gate, g_norm)


ATT_TQ = 256
ATT_TK = 1024
FOX_TQ = 512
FOX_TK = 1024


def _half_mask(shape, half):
    lane = lax.broadcasted_iota(jnp.int32, shape, len(shape) - 1)
    return (lane < HEAD_DIM) if half == 0 else (lane >= HEAD_DIM)


def _flash_step(s, v2, m, l, acc):
    m_new = jnp.maximum(m, jnp.max(s, axis=-1, keepdims=True))
    alpha = jnp.exp(m - m_new)
    p = jnp.exp(s - m_new)
    l = alpha * l + jnp.sum(p, axis=-1, keepdims=True)
    acc = alpha * acc + jnp.dot(p.astype(BF16), v2, preferred_element_type=F32)
    return m_new, l, acc


QK_SCALE = HEAD_DIM ** -0.5


def _qk(qm, k2):
    return lax.dot_general(qm, k2, (((1,), (1,)), ((), ())), preferred_element_type=F32)


def _load_kv(b, seq, srcs, dsts, sems):
    copies = []
    for n, (src, col, width) in enumerate(srcs):
        cp = pltpu.make_async_copy(src.at[pl.ds(b * seq, seq), pl.ds(col, width)], dsts[n], sems.at[n])
        cp.start()
        copies.append(cp)
    for cp in copies:
        cp.wait()


def _fcum_kernel(ex_ref, bf_ref, o_ref, carry_ref):
    i = pl.program_id(1)

    @pl.when(i == 0)
    def _():
        carry_ref[...] = jnp.zeros_like(carry_ref)

    lf = _log_sigmoid(ex_ref[...] + bf_ref[...])
    F = _cumsum_rows(lf) + carry_ref[...]
    o_ref[...] = F
    carry_ref[...] = F[F.shape[0] - 1:, :]


def _forget_cumsum(extra, b_forget_row, batch, seq, tb=256):
    nblk = seq // tb
    return pl.pallas_call(
        _fcum_kernel,
        grid=(batch, nblk),
        in_specs=[pl.BlockSpec((tb, 128), lambda b, i: (b * nblk + i, 0)),
                  pl.BlockSpec((1, 128), lambda b, i: (0, 0))],
        out_specs=pl.BlockSpec((tb, 128), lambda b, i: (b * nblk + i, 0)),
        out_shape=jax.ShapeDtypeStruct((batch * seq, 128), F32),
        scratch_shapes=[pltpu.VMEM((1, 128), F32)],
        compiler_params=_cparams(("parallel", "arbitrary")),
        name="forget_cumsum",
    )(extra, b_forget_row)


def _fox_kernel(q_ref, fq_ref, fk_ref, kv_hbm, o_ref, k_vm, v_vm, sems, *, seq):
    b = pl.program_id(0)
    i = pl.program_id(1)
    TQ, TK = FOX_TQ, FOX_TK

    @pl.when(i == 0)
    def _():
        _load_kv(b, seq, [(kv_hbm, CD_KC, 512), (kv_hbm, CD_VC, 512)], [k_vm, v_vm], sems)

    nj = (i * TQ + TQ - 1) // TK + 1
    row = i * TQ + lax.broadcasted_iota(jnp.int32, (TQ, TK), 0)
    col0 = lax.broadcasted_iota(jnp.int32, (TQ, TK), 1)
    for p in range(FOX_HEADS // 2):
        q2 = q_ref[:, p * 128:(p + 1) * 128]
        qms = [jnp.where(_half_mask((TQ, 128), half), q2 * QK_SCALE, jnp.zeros_like(q2)) for half in range(2)]
        fqs = [fq_ref[:, 2 * p + half:2 * p + half + 1] for half in range(2)]

        def body(j, carry, diagonal=False, qms=qms, fqs=fqs, p=p):
            k0 = pl.multiple_of(j * TK, TK)
            k2 = k_vm[pl.ds(k0, TK), p * 128:(p + 1) * 128]
            v2 = v_vm[pl.ds(k0, TK), p * 128:(p + 1) * 128]
            out = []
            for half in range(2):
                m, l, acc = carry[half]
                s = _qk(qms[half], k2) + (fqs[half] - fk_ref[2 * p + half, pl.ds(j, 1), :])
                if diagonal:
                    s = jnp.where(col0 + j * TK <= row, s, MASKED)
                out.append(_flash_step(s, v2, m, l, acc))
            return tuple(out)

        init = (jnp.full((TQ, 1), M_INIT, F32), jnp.zeros((TQ, 1), F32), jnp.zeros((TQ, 128), F32))
        carry = lax.fori_loop(0, nj - 1, body, (init, init))
        (_, l0, acc0), (_, l1, acc1) = body(nj - 1, carry, diagonal=True)
        o_ref[:, p * 128:(p + 1) * 128] = jnp.where(_half_mask((TQ, 128), 0), acc0 / l0, acc1 / l1).astype(o_ref.dtype)


def _fox(main, fcum, batch, seq):
    TQ, TK = FOX_TQ, FOX_TK
    nq = seq // TQ
    nk = seq // TK
    n = batch * seq
    fk = fcum[:, :FOX_HEADS].reshape(batch, seq, FOX_HEADS).transpose(0, 2, 1).reshape(batch * FOX_HEADS, nk, TK)
    return pl.pallas_call(
        functools.partial(_fox_kernel, seq=seq),
        grid=(batch, nq),
        in_specs=[pl.BlockSpec((TQ, 512), lambda b, i: (b * nq + i, CD_QC // 512)),
                  pl.BlockSpec((TQ, 128), lambda b, i: (b * nq + i, 0)),
                  pl.BlockSpec((FOX_HEADS, nk, TK), lambda b, i: (b, 0, 0)),
                  pl.BlockSpec(memory_space=pl.ANY)],
        out_specs=pl.BlockSpec((TQ, 512), lambda b, i: (b * nq + i, 0)),
        out_shape=jax.ShapeDtypeStruct((n, 512), BF16),
        scratch_shapes=[pltpu.VMEM((seq, 512), BF16), pltpu.VMEM((seq, 512), BF16),
                        pltpu.SemaphoreType.DMA((2,))],
        compiler_params=_cparams(("arbitrary", "arbitrary"), VMEM_LIMIT),
        name="fox_attention",
    )(main, fcum, fk, main)


INT_MIN = -(2 ** 31)
N_DELTA = 18


def _sortable(x):
    b = pltpu.bitcast(x, jnp.int32)
    return b ^ ((b >> 31) & 0x7FFFFFFF)


def _dsa_kernel(q_ref, qi_ref, ex_ref, kv_hbm, kid_hbm, tz_hbm, o_ref,
                k_vm, v_vm, kid_vm, tz_vm, key_vm, qm_vm, qim_vm, mask_vm, m_vm, l_vm, acc_vm, cut_vm, sems,
                *, seq, topk):
    b = pl.program_id(0)
    i = pl.program_id(1)
    TQ, TK = ATT_TQ, ATT_TK

    @pl.when(i == 0)
    def _():
        _load_kv(b, seq, [(kv_hbm, AB_KB, 512), (kv_hbm, AB_VB, 512), (kid_hbm, 0, 128)],
                 [k_vm, v_vm, kid_vm], sems)

    @pl.when((i == 0) & (b == 0))
    def _():
        cp = pltpu.make_async_copy(tz_hbm, tz_vm, sems.at[3])
        cp.start()
        cp.wait()

    nj = (i * TQ + TQ - 1) // TK + 1
    row = i * TQ + lax.broadcasted_iota(jnp.int32, (TQ, TK), 0)
    col0 = lax.broadcasted_iota(jnp.int32, (TQ, TK), 1)

    for p in range(4):
        q2 = q_ref[:, p * 128:(p + 1) * 128]
        qi2 = qi_ref[:, p * 128:(p + 1) * 128]
        for half in range(2):
            keep = _half_mask((TQ, 128), half)
            qm_vm[2 * p + half] = jnp.where(keep, q2 * QK_SCALE, jnp.zeros_like(q2))
            qim_vm[2 * p + half] = jnp.where(keep, qi2, jnp.zeros_like(qi2))

    def score_chunk(j, carry):
        k0 = pl.multiple_of(j * TK, TK)
        kd = kid_vm[pl.ds(k0, TK), :].astype(BF16)
        score = jnp.zeros((TQ, TK), F32)
        for h in range(IDX_HEADS):
            s = lax.dot_general(qim_vm[h], kd, (((1,), (1,)), ((), ())), preferred_element_type=F32)
            w = ex_ref[:, EX_WI + h:EX_WI + h + 1]
            score = score + w * jnp.maximum(s, 0.0)
        score = jnp.where(col0 + j * TK <= row, score, -jnp.inf)
        key_vm[j] = _sortable(score)
        return carry

    lax.fori_loop(0, nj, score_chunk, 0)

    RH = 128
    assert seq // LANES <= 256
    ones_b = jnp.ones((LANES, LANES), BF16)

    def count_rows(hit_fn):
        parts = []
        for r0 in range(0, TQ, RH):
            def body(j, acc, r0=r0):
                for cb in range(TK // LANES):
                    blk = key_vm[j, r0:r0 + RH, cb * LANES:(cb + 1) * LANES]
                    acc = acc + jnp.where(hit_fn(blk, j, r0, cb), 1, 0)
                return acc
            parts.append(lax.fori_loop(0, nj, body, jnp.zeros((RH, LANES), jnp.int32)))
        acc = jnp.concatenate(parts, axis=0)
        tot = jnp.dot(acc.astype(F32).astype(BF16), ones_b, preferred_element_type=F32)
        return tot.astype(jnp.int32)

    def count_ge(c):
        return count_rows(lambda blk, j, r0, cb: blk >= c[r0:r0 + RH])

    ans = jnp.where(count_ge(jnp.zeros((TQ, LANES), jnp.int32)) >= topk, 0, INT_MIN).astype(jnp.int32)

    def bit_step(it, ans):
        cand = ans | (jnp.int32(1) << (30 - it))
        return jnp.where(count_ge(cand) >= topk, cand, ans)

    thr_d = lax.fori_loop(0, 31, bit_step, ans)
    thr = thr_d[:, 0:1]

    cut_vm[...] = jnp.full((TQ, 1), seq, jnp.int32)
    n_ge = count_ge(thr_d)

    @pl.when(jnp.max(n_ge) > topk)
    def _():
        need = topk - count_ge(thr_d + 1)
        colh = lax.broadcasted_iota(jnp.int32, (RH, LANES), 1)

        def count_eq_below(x):
            return count_rows(lambda blk, j, r0, cb: (blk == thr_d[r0:r0 + RH])
                              & (colh + (j * TK + cb * LANES) < x[r0:r0 + RH]))

        nbits = max(1, (seq - 1).bit_length())

        def cut_step(it, x):
            cand = x | (jnp.int32(1) << (nbits - 1 - it))
            return jnp.where(count_eq_below(cand) < need, cand, x)

        cut_d = lax.fori_loop(0, nbits, cut_step, jnp.zeros((TQ, LANES), jnp.int32))
        cut_vm[...] = cut_d[:, 0:1]

    cut = cut_vm[...]

    m_vm[...] = jnp.full(m_vm.shape, M_INIT, F32)
    l_vm[...] = jnp.zeros(l_vm.shape, F32)
    acc_vm[...] = jnp.zeros(acc_vm.shape, F32)

    def attend(j, carry):
        k0 = pl.multiple_of(j * TK, TK)
        col = col0 + j * TK
        key = key_vm[j]
        sel = ((key > thr) | ((key == thr) & (col <= cut))) & (col <= row)
        mask_vm[...] = jnp.where(sel, 0.0, MASKED)
        for h in range(DSA_HEADS):
            p = h // 2
            k2 = k_vm[pl.ds(k0, TK), p * 128:(p + 1) * 128]
            v2 = v_vm[pl.ds(k0, TK), p * 128:(p + 1) * 128]
            tiles = []
            for ra in range(TQ // 128):
                rowt = []
                for cb in range(TK // 128):
                    delta = jnp.clip(i * (TQ // 128) + ra - j * (TK // 128) - cb, 0, N_DELTA - 1)
                    rowt.append(tz_vm[delta, h])
                tiles.append(jnp.concatenate(rowt, axis=1))
            bias = jnp.concatenate(tiles, axis=0)
            s = _qk(qm_vm[h], k2) + bias + mask_vm[...]
            m, l, acc = _flash_step(s, v2, m_vm[h], l_vm[h], acc_vm[h])
            m_vm[h] = m
            l_vm[h] = l
            acc_vm[h] = acc
        return carry

    lax.fori_loop(0, nj, attend, 0)

    for p in range(4):
        o0 = acc_vm[2 * p] / l_vm[2 * p]
        o1 = acc_vm[2 * p + 1] / l_vm[2 * p + 1]
        o_ref[:, p * 128:(p + 1) * 128] = jnp.where(_half_mask((TQ, 128), 0), o0, o1).astype(o_ref.dtype)


def _dsa(main, extra, toeplitz, batch, seq):
    TQ, TK = ATT_TQ, ATT_TK
    nq = seq // TQ
    nk = seq // TK
    n = batch * seq
    topk = min(DSA_TOPK_MAX, seq // 4)
    return pl.pallas_call(
        functools.partial(_dsa_kernel, seq=seq, topk=topk),
        grid=(batch, nq),
        in_specs=[pl.BlockSpec((TQ, 512), lambda b, i: (b * nq + i, AB_QB // 512)),
                  pl.BlockSpec((TQ, 512), lambda b, i: (b * nq + i, AB_QI // 512)),
                  pl.BlockSpec((TQ, 128), lambda b, i: (b * nq + i, 1)),
                  pl.BlockSpec(memory_space=pl.ANY),
                  pl.BlockSpec(memory_space=pl.ANY),
                  pl.BlockSpec(memory_space=pl.ANY)],
        out_specs=pl.BlockSpec((TQ, 512), lambda b, i: (b * nq + i, 0)),
        out_shape=jax.ShapeDtypeStruct((n, 512), BF16),
        scratch_shapes=[pltpu.VMEM((seq, 512), BF16), pltpu.VMEM((seq, 512), BF16),
                        pltpu.VMEM((seq, 128), F32),
                        pltpu.VMEM((N_DELTA, DSA_HEADS, 128, 128), F32),
                        pltpu.VMEM((nk, TQ, TK), jnp.int32),
                        pltpu.VMEM((DSA_HEADS, TQ, 128), BF16), pltpu.VMEM((IDX_HEADS, TQ, 128), BF16),
                        pltpu.VMEM((TQ, TK), F32),
                        pltpu.VMEM((DSA_HEADS, TQ, 1), F32), pltpu.VMEM((DSA_HEADS, TQ, 1), F32),
                        pltpu.VMEM((DSA_HEADS, TQ, 128), F32),
                        pltpu.VMEM((TQ, 1), jnp.int32),
                        pltpu.SemaphoreType.DMA((4,))],
        compiler_params=_cparams(("arbitrary", "arbitrary"), VMEM_LIMIT),
        name="dsa_attention",
    )(main, main, extra, main, extra, toeplitz)


DIL_TQ = 128


def _dil_kernel(q_ref, kp_ref, kc_ref, vp_ref, vc_ref, bm_ref, o_ref, ld_ref):
    i = pl.program_id(2)
    TQ = DIL_TQ
    col = lax.broadcasted_iota(jnp.int32, (TQ, 2 * TQ), 1)
    first = jnp.where((i == 0) & (col < TQ), MASKED, 0.0)
    for p in range(DIL_HEADS // 2):
        sl = slice(p * 128, (p + 1) * 128)
        q2 = q_ref[:, sl]
        kc = jnp.concatenate([kp_ref[:, sl], kc_ref[:, sl]], axis=0)
        vc = jnp.concatenate([vp_ref[:, sl], vc_ref[:, sl]], axis=0)
        outs, lds = [], []
        for half in range(2):
            h = 2 * p + half
            qm = jnp.where(_half_mask((TQ, 128), half), q2 * QK_SCALE, jnp.zeros_like(q2))
            lg = _qk(qm, kc) + bm_ref[h] + first
            m = jnp.max(lg, axis=-1, keepdims=True)
            e = jnp.exp(lg - m)
            s = jnp.sum(e, axis=-1, keepdims=True)
            outs.append(jnp.dot(e.astype(BF16), vc, preferred_element_type=F32) / s)
            lds.append(jnp.broadcast_to(m + jnp.log(s), (TQ, 128)))
        lo = _half_mask((TQ, 128), 0)
        o_ref[:, sl] = jnp.where(lo, outs[0], outs[1])
        ld_ref[:, sl] = jnp.where(lo, lds[0], lds[1])


def _dilated_pattern(main, biasmask, dil, batch, seq):
    TQ = DIL_TQ
    n = batch * seq
    ld_seq = seq // dil
    nblk = ld_seq // TQ
    view = main.reshape(n // dil, dil * CD_MAIN)
    cpb = CD_MAIN // 512

    def qmap(b, r, i):
        return (b * nblk + i, r * cpb + CD_QD // 512)

    def kmap(off):
        def f(b, r, i):
            return (b * nblk + jnp.maximum(i - 1 + off, 0), r * cpb + CD_KD // 512)
        return f

    def vmap_(off):
        def f(b, r, i):
            return (b * nblk + jnp.maximum(i - 1 + off, 0), r * cpb + CD_VD // 512)
        return f

    o, ld = pl.pallas_call(
        _dil_kernel,
        grid=(batch, dil, nblk),
        in_specs=[pl.BlockSpec((TQ, 512), qmap),
                  pl.BlockSpec((TQ, 512), kmap(0)), pl.BlockSpec((TQ, 512), kmap(1)),
                  pl.BlockSpec((TQ, 512), vmap_(0)), pl.BlockSpec((TQ, 512), vmap_(1)),
                  pl.BlockSpec((DIL_HEADS, TQ, 2 * TQ), lambda b, r, i: (0, 0, 0))],
        out_specs=[pl.BlockSpec((TQ, 512), lambda b, r, i: (b * nblk + i, r)),
                   pl.BlockSpec((TQ, 512), lambda b, r, i: (b * nblk + i, r))],
        out_shape=[jax.ShapeDtypeStruct((n // dil, dil * 512), F32),
                   jax.ShapeDtypeStruct((n // dil, dil * 512), F32)],
        compiler_params=_cparams(("parallel", "parallel", "parallel"), VMEM_LIMIT),
        name="dilated_attention",
    )(view, view, view, view, view, biasmask)
    return o.reshape(n, 512), ld.reshape(n, 512)


def _dil_merge_kernel(o0, l0, o1, l1, o2, l2, out_ref):
    a, b, c = l0[...], l1[...], l2[...]
    mx = jnp.maximum(jnp.maximum(a, b), c)
    ea, eb, ec = jnp.exp(a - mx), jnp.exp(b - mx), jnp.exp(c - mx)
    tot = ea + eb + ec
    out_ref[...] = ((ea * o0[...] + eb * o1[...] + ec * o2[...]) / tot).astype(out_ref.dtype)


def _dilated(main, bvec, batch, seq):
    n = batch * seq
    parts = []
    t = DIL_TQ
    pad = jnp.full((REL_HEADS, t - 1), MASKED, F32)
    for window, dil in DIL_PATTERNS:
        assert window // dil == t
        steps = bvec[:, BVEC_SHIFT:BVEC_SHIFT + dil * (t + 1):dil]
        w = jnp.flip(jnp.concatenate([pad, steps, pad], axis=1), axis=-1)
        w = jnp.concatenate([w, w[:, :1]], axis=-1)
        skew = jnp.tile(w, (1, t))[:, :t * (3 * t - 1)].reshape(REL_HEADS, t, 3 * t - 1)
        biasmask = skew[:, :, t - 1:3 * t - 1]
        parts.extend(_dilated_pattern(main, biasmask, dil, batch, seq))
    tm = 512
    spec = pl.BlockSpec((tm, 512), lambda i: (i, 0))
    return pl.pallas_call(
        _dil_merge_kernel,
        grid=(n // tm,),
        in_specs=[spec] * 6,
        out_specs=spec,
        out_shape=jax.ShapeDtypeStruct((n, 512), BF16),
        compiler_params=_cparams(("parallel",)),
        name="dilated_merge",
    )(*parts)


MOE_TM = 512
MOE_TF = 512


def _split3(a):
    hi = a.astype(BF16)
    r1 = a - hi.astype(F32)
    mid = r1.astype(BF16)
    lo = (r1 - mid.astype(F32)).astype(BF16)
    return hi, mid, lo


def _router_kernel(x_ref, w_ref, idx_ref, gate_ref):
    xh, xm, xl = _split3(x_ref[...])
    wh, wm, wl = _split3(w_ref[...])
    dot = functools.partial(jnp.dot, preferred_element_type=F32)
    logits = (dot(xh, wh) + (dot(xh, wm) + dot(xm, wh))
              + (dot(xh, wl) + dot(xm, wm) + dot(xl, wh)))
    lane = lax.broadcasted_iota(jnp.int32, logits.shape, 1)
    lg = jnp.where(lane < N_EXPERTS, logits, -jnp.inf)
    v1 = jnp.max(lg, axis=-1, keepdims=True)
    i1 = jnp.min(jnp.where(lg == v1, lane, LANES), axis=-1, keepdims=True)
    lg2 = jnp.where(lane == i1, -jnp.inf, lg)
    v2 = jnp.max(lg2, axis=-1, keepdims=True)
    i2 = jnp.min(jnp.where(lg2 == v2, lane, LANES), axis=-1, keepdims=True)
    e2 = jnp.exp(v2 - v1)
    den = 1.0 + e2
    idx_ref[...] = jnp.where(lane == 0, i1, jnp.where(lane == 1, i2, 0))
    gate_ref[...] = jnp.where(lane == 0, 1.0 / den, jnp.where(lane == 1, e2 / den, 0.0))


def _router(x, w_router_pad, tm=512):
    n, d = x.shape
    return pl.pallas_call(
        _router_kernel,
        grid=(n // tm,),
        in_specs=[pl.BlockSpec((tm, d), lambda i: (i, 0)),
                  pl.BlockSpec((d, LANES), lambda i: (0, 0))],
        out_specs=[pl.BlockSpec((tm, LANES), lambda i: (i, 0)),
                   pl.BlockSpec((tm, LANES), lambda i: (i, 0))],
        out_shape=[jax.ShapeDtypeStruct((n, LANES), jnp.int32),
                   jax.ShapeDtypeStruct((n, LANES), F32)],
        compiler_params=_cparams(("parallel",), VMEM_LIMIT),
        name="moe_router",
    )(x, w_router_pad)


MOE_TC = 512
MOE_WIN = 640


def _moe_ffn_kernel(te_ref, nv_ref, clo_ref, chi_ref, src_ref, x_hbm, w1_ref, w3_ref, w2_ref, gate_ref,
                    hi_ref, lo_ref, acc_ref, xs_ref, xbuf, sems):
    t = pl.program_id(0)
    f = pl.program_id(1)
    last = pl.num_programs(1) - 1
    live = t < nv_ref[0]
    tm = acc_ref.shape[0]

    def fetch(c, slot):
        return pltpu.make_async_copy(x_hbm.at[pl.ds(c * MOE_TC, MOE_TC)], xbuf.at[slot], sems.at[slot])

    @pl.when(live & (f == 0))
    def _():
        clo = clo_ref[t]
        chi = chi_ref[t]
        acc_ref[...] = jnp.zeros_like(acc_ref)
        fetch(clo, 0).start()
        src = src_ref[...]
        lane = lax.broadcasted_iota(jnp.int32, (tm, MOE_TC), 1)

        def body(c, carry):
            slot = (c - clo) & 1

            @pl.when(c < chi)
            def _():
                fetch(c + 1, 1 - slot).start()

            fetch(c, slot).wait()
            onehot = jnp.where(src == lane + c * MOE_TC, 1.0, 0.0).astype(BF16)
            acc_ref[...] += jnp.dot(onehot, xbuf[slot], preferred_element_type=F32)
            return carry

        lax.fori_loop(clo, chi + 1, body, 0)
        xs_ref[...] = acc_ref[...].astype(BF16)

    @pl.when(f == 0)
    def _():
        acc_ref[...] = jnp.zeros_like(acc_ref)

    @pl.when(live)
    def _():
        xb = xs_ref[...]
        a = jnp.dot(xb, w1_ref[...], preferred_element_type=F32)
        c = jnp.dot(xb, w3_ref[...], preferred_element_type=F32)
        h = (_silu(a) * c).astype(BF16)
        acc_ref[...] += jnp.dot(h, w2_ref[...], preferred_element_type=F32)

    @pl.when(f == last)
    def _():
        y = acc_ref[...] * gate_ref[...]
        hi = y.astype(BF16)
        hi_ref[...] = hi
        lo_ref[...] = (y - hi.astype(F32)).astype(BF16)


def _moe_ffn(xb, row_src, w1, w3, w2, gate_sorted, tile_expert, n_live, chunk_lo, chunk_hi):
    p_rows = row_src.shape[0]
    d = xb.shape[1]
    ff = w1.shape[2]
    tm, tf = MOE_TM, MOE_TF

    def rows(t, f, te, nv, clo, chi):
        return (t, 0)

    def wf(t, f, nv):
        return jnp.where(t < nv[0], f, ff // tf - 1)

    return pl.pallas_call(
        _moe_ffn_kernel,
        grid_spec=pltpu.PrefetchScalarGridSpec(
            num_scalar_prefetch=4,
            grid=(p_rows // tm, ff // tf),
            in_specs=[pl.BlockSpec((tm, 1), rows),
                      pl.BlockSpec(memory_space=pl.ANY),
                      pl.BlockSpec((None, d, tf), lambda t, f, te, nv, clo, chi: (te[t], 0, wf(t, f, nv))),
                      pl.BlockSpec((None, d, tf), lambda t, f, te, nv, clo, chi: (te[t], 0, wf(t, f, nv))),
                      pl.BlockSpec((None, tf, d), lambda t, f, te, nv, clo, chi: (te[t], wf(t, f, nv), 0)),
                      pl.BlockSpec((tm, 1), rows)],
            out_specs=[pl.BlockSpec((tm, d), rows), pl.BlockSpec((tm, d), rows)],
            scratch_shapes=[pltpu.VMEM((tm, d), F32), pltpu.VMEM((tm, d), BF16),
                            pltpu.VMEM((2, MOE_TC, d), BF16), pltpu.SemaphoreType.DMA((2,))],
        ),
        out_shape=[jax.ShapeDtypeStruct((p_rows, d), BF16), jax.ShapeDtypeStruct((p_rows, d), BF16)],
        compiler_params=_cparams(("arbitrary", "arbitrary"), VMEM_LIMIT),
        name="moe_grouped_swiglu",
    )(tile_expert, n_live, chunk_lo, chunk_hi, row_src, xb, w1, w3, w2, gate_sorted)


def _combine_ln_kernel(win_ref, pos_ref, eid_ref, hi_hbm, lo_hbm, x_ref, g_ref, b_ref, xf_ref, xb_ref,
                       acc_ref, hbuf, lbuf, sems):
    blk = pl.program_id(0)
    tm = acc_ref.shape[0]

    def fetch(e, slot):
        w = pl.multiple_of(win_ref[blk * N_EXPERTS + e], 16)
        return (pltpu.make_async_copy(hi_hbm.at[pl.ds(w, MOE_WIN)], hbuf.at[slot], sems.at[0, slot]),
                pltpu.make_async_copy(lo_hbm.at[pl.ds(w, MOE_WIN)], lbuf.at[slot], sems.at[1, slot]))

    for cp in fetch(0, 0):
        cp.start()
    pos0 = pos_ref[:, 0:1]
    pos1 = pos_ref[:, 1:2]
    lane = lax.broadcasted_iota(jnp.int32, (tm, MOE_WIN), 1)
    acc_ref[...] = ALPHA * x_ref[...]
    for e in range(N_EXPERTS):
        slot = e & 1
        if e + 1 < N_EXPERTS:
            for cp in fetch(e + 1, 1 - slot):
                cp.start()
        for cp in fetch(e, slot):
            cp.wait()
        w = win_ref[blk * N_EXPERTS + e]
        rel0 = jnp.where(eid_ref[:, 0:1] == e, pos0 - w, -1)
        rel1 = jnp.where(eid_ref[:, 1:2] == e, pos1 - w, -1)
        onehot = jnp.where((lane == rel0) | (lane == rel1), 1.0, 0.0).astype(BF16)
        acc_ref[...] += (jnp.dot(onehot, hbuf[slot], preferred_element_type=F32)
                         + jnp.dot(onehot, lbuf[slot], preferred_element_type=F32))
    y = _layernorm_rows(acc_ref[...], g_ref[...], b_ref[...])
    xf_ref[...] = y
    xb_ref[...] = y.astype(BF16)


def _combine_ln(win, pos2, eid2, ys_hi, ys_lo, x, g, b):
    n, d = x.shape
    tm = MOE_TC
    return pl.pallas_call(
        _combine_ln_kernel,
        grid_spec=pltpu.PrefetchScalarGridSpec(
            num_scalar_prefetch=1,
            grid=(n // tm,),
            in_specs=[pl.BlockSpec((tm, TOP_K), lambda i, w: (i, 0)),
                      pl.BlockSpec((tm, TOP_K), lambda i, w: (i, 0)),
                      pl.BlockSpec(memory_space=pl.ANY),
                      pl.BlockSpec(memory_space=pl.ANY),
                      pl.BlockSpec((tm, d), lambda i, w: (i, 0)),
                      pl.BlockSpec((1, d), lambda i, w: (0, 0)),
                      pl.BlockSpec((1, d), lambda i, w: (0, 0))],
            out_specs=[pl.BlockSpec((tm, d), lambda i, w: (i, 0)),
                       pl.BlockSpec((tm, d), lambda i, w: (i, 0))],
            scratch_shapes=[pltpu.VMEM((tm, d), F32),
                            pltpu.VMEM((2, MOE_WIN, d), BF16), pltpu.VMEM((2, MOE_WIN, d), BF16),
                            pltpu.SemaphoreType.DMA((2, 2))],
        ),
        out_shape=[jax.ShapeDtypeStruct((n, d), F32), jax.ShapeDtypeStruct((n, d), BF16)],
        compiler_params=_cparams(("arbitrary",), VMEM_LIMIT),
        name="moe_combine_ln",
    )(win, pos2, eid2, ys_hi, ys_lo, x, g, b)


def _moe(x, xb, w_router_pad, w1, w3, w2, g, b):
    n, d = x.shape
    tm = MOE_TM
    idx, gates = _router(x, w_router_pad)
    e_flat = idx[:, :TOP_K].reshape(-1)
    g_flat = gates[:, :TOP_K].reshape(-1)
    onehot = (e_flat[:, None] == jnp.arange(N_EXPERTS, dtype=jnp.int32)[None, :]).astype(jnp.int32)
    csum = jnp.cumsum(onehot, axis=0)
    rank = jnp.take_along_axis(csum, e_flat[:, None], axis=1)[:, 0] - 1
    counts = csum[-1]
    padded = ((counts + tm - 1) // tm) * tm
    ends = jnp.cumsum(padded)
    pos = ((ends - padded)[e_flat] + rank).astype(jnp.int32)
    tiles = (TOP_K * n) // tm + N_EXPERTS + 2
    p_rows = tiles * tm
    token = jnp.arange(TOP_K * n, dtype=jnp.int32) // TOP_K
    packed = jnp.stack([token, lax.bitcast_convert_type(g_flat, jnp.int32)], axis=1)
    empty = jnp.broadcast_to(jnp.array([-1, 0], jnp.int32), (p_rows, 2))
    placed = empty.at[pos].set(packed)
    row_src = placed[:, 0]
    gate_sorted = lax.bitcast_convert_type(placed[:, 1], F32).reshape(p_rows, 1)
    tile_start = jnp.arange(tiles, dtype=jnp.int32) * tm
    tile_expert = jnp.minimum(jnp.searchsorted(ends, tile_start, side="right"), N_EXPERTS - 1).astype(jnp.int32)
    n_live = (ends[-1] // tm).astype(jnp.int32).reshape(1)
    tile_expert = jnp.where(tile_start < ends[-1], tile_expert, tile_expert[n_live[0] - 1])
    per_tile = row_src.reshape(tiles, tm)
    chunk_lo = (jnp.min(jnp.where(per_tile >= 0, per_tile, n - 1), axis=1) // MOE_TC).astype(jnp.int32)
    chunk_hi = (jnp.max(per_tile, axis=1) // MOE_TC).astype(jnp.int32)
    big = jnp.int32(p_rows)
    wkey = (token // MOE_TC) * N_EXPERTS + e_flat
    win = jnp.full(((n // MOE_TC) * N_EXPERTS,), big, jnp.int32).at[wkey].min(pos)
    win = (jnp.where(win == big, 0, win) // 16) * 16

    ys_hi, ys_lo = _moe_ffn(xb, row_src.reshape(p_rows, 1), w1, w3, w2, gate_sorted, tile_expert, n_live,
                            chunk_lo, chunk_hi)
    return _combine_ln(win, pos.reshape(n, TOP_K), idx[:, :TOP_K], ys_hi, ys_lo, x, g, b)


def _regroup_ab(w):
    main = jnp.concatenate([w[:, 0:1536], w[:, 1552:3600]], axis=1)
    ki = w[:, 3600:3664]
    extra = jnp.concatenate([ki, ki, w[:, 1536:1552], w[:, 3664:3672],
                             jnp.zeros((w.shape[0], 128 - GLA_GATE_RANK - IDX_HEADS), w.dtype)], axis=1)
    return main.astype(BF16), extra.astype(BF16)


def _regroup_cd(w):
    main = jnp.concatenate([w[:, 0:1536], w[:, 1544:3080]], axis=1)
    extra = jnp.concatenate([w[:, 1536:1544], jnp.zeros((w.shape[0], 128 - FOX_HEADS), w.dtype)], axis=1)
    return main.astype(BF16), extra.astype(BF16)


def _toeplitz_tiles(bvec):
    t = 128
    seg = jnp.stack([bvec[:, dl * t:dl * t + 2 * t - 1] for dl in range(N_DELTA - 1)], axis=1)
    w = jnp.flip(seg, axis=-1)
    w = jnp.concatenate([w, w[..., :1]], axis=-1)
    skew = jnp.tile(w, (1, 1, t))[..., :t * (2 * t - 1)].reshape(REL_HEADS, N_DELTA - 1, t, 2 * t - 1)
    near = skew[..., t - 1:]
    far = jnp.broadcast_to(bvec[:, BVEC_LEN - 1][:, None, None, None], (REL_HEADS, 1, t, t))
    return jnp.concatenate([near, far], axis=1).transpose(1, 0, 2, 3)


def kernel(x, ln_g, ln_b, rel_table, w_in_ab, w_gate_a, b_gate_a, g_norm_a, w_out_ab, w_in_cd, b_forget,
           w_out_cd, w1_dense, w3_dense, w2_dense, w_router, w1_moe, w3_moe, w2_moe):
    batch, seq, d = x.shape
    n = batch * seq
    xf = x.reshape(n, d)
    xb = xf.astype(BF16)

    bvec = _bias_vector(rel_table)
    toeplitz = _toeplitz_tiles(bvec)

    for layer in range(DEPTH):
        j = layer // 2
        g0, b0 = ln_g[layer, 0][None, :], ln_b[layer, 0][None, :]
        g1, b1 = ln_g[layer, 1][None, :], ln_b[layer, 1][None, :]
        if layer % 2 == 0:
            w_main, w_extra = _regroup_ab(w_in_ab[j])
            main = _matmul(xb, w_main, BF16, 512, w_main.shape[1])
            extra = _matmul(xb, w_extra, F32, 512, AB_EXTRA)
            oa = _gla(main, extra, w_gate_a[j], b_gate_a[j][None, :], g_norm_a[j][None, :], batch, seq)
            ob = _dsa(main, extra, toeplitz, batch, seq)
            xf, xb = _outproj_ln(oa, ob, w_out_ab[j].astype(BF16), xf, g0, b0)
            xf, xb = _swiglu_ln(xb, w1_dense[j].astype(BF16), w3_dense[j].astype(BF16),
                                w2_dense[j].astype(BF16), xf, g1, b1)
        else:
            w_main, w_extra = _regroup_cd(w_in_cd[j])
            main = _matmul(xb, w_main, BF16, 512, w_main.shape[1])
            extra = _matmul(xb, w_extra, F32, 512, CD_EXTRA)
            bf_row = jnp.concatenate([b_forget[j], jnp.zeros((128 - FOX_HEADS,), F32)])[None, :]
            fcum = _forget_cumsum(extra, bf_row, batch, seq)
            oc = _fox(main, fcum, batch, seq)
            od = _dilated(main, bvec, batch, seq)
            xf, xb = _outproj_ln(oc, od, w_out_cd[j].astype(BF16), xf, g0, b0)
            w_r = jnp.concatenate([w_router[j], jnp.zeros((d, LANES - N_EXPERTS), F32)], axis=1)
            xf, xb = _moe(xf, xb, w_r, w1_moe[j].astype(BF16), w3_moe[j].astype(BF16),
                          w2_moe[j].astype(BF16), g1, b1)
    return xf.reshape(batch, seq, d)
```
